```python
import math
import jax, jax.numpy as jnp
from jax import lax
import numpy as np

D_MODEL = 1024
BATCH = 8
SEQ = 4096
DEPTH = 4

CHUNK = 128

MLSTM_HEADS = 4
MLSTM_INNER = D_MODEL // 2
MLSTM_V = MLSTM_INNER // MLSTM_HEADS
MLSTM_QK = MLSTM_V // 2
RET_HEADS = 4
RET_INNER = D_MODEL // 2
RET_V = RET_INNER // RET_HEADS
RET_QK = RET_V // 2
ROPE_BASE = 10000.0
HYB_INNER = MLSTM_INNER + RET_INNER
HYB_SIZES = (MLSTM_HEADS * MLSTM_QK, MLSTM_HEADS * MLSTM_QK, MLSTM_INNER, MLSTM_HEADS, MLSTM_HEADS, MLSTM_INNER,
             RET_HEADS * RET_QK, RET_HEADS * RET_QK, RET_INNER, RET_INNER)
HYB_PROJ = sum(HYB_SIZES)
HYB_SPLITS = tuple(sum(HYB_SIZES[:i + 1]) for i in range(len(HYB_SIZES) - 1))

SSD_INNER = 2 * D_MODEL
SSD_HEADDIM = 64
SSD_HEADS = SSD_INNER // SSD_HEADDIM
SSD_GROUPS = 8
SSD_STATE = 128
SSD_CONV = 4
SSD_CONV_DIM = SSD_INNER + 2 * SSD_GROUPS * SSD_STATE
SSD_PROJ = SSD_INNER + SSD_CONV_DIM + SSD_HEADS

D_FF = 4 * D_MODEL

kernel_name = "hybrid_mlstm_retention_ssd_trunk"

F32 = jnp.float32


def rmsnorm(x, w, eps=1e-6):
    xf = x.astype(F32)
    y = xf * lax.rsqrt(jnp.mean(xf * xf, axis=-1, keepdims=True) + eps)
    return (y * w.astype(F32)).astype(x.dtype)


def headnorm(h, w, eps=1e-5):
    mu = jnp.mean(h, axis=-1, keepdims=True)
    var = jnp.mean(jnp.square(h - mu), axis=-1, keepdims=True)
    y = (h - mu) * lax.rsqrt(var + eps)
    bsz, seq, nh, d = h.shape
    return y.reshape(bsz, seq, nh * d) * w.astype(F32)


def rotary(t):
    seq, d = t.shape[1], t.shape[-1]
    inv = ROPE_BASE ** (-jnp.arange(0, d, 2, dtype=F32) / d)
    ang = jnp.arange(seq, dtype=F32)[:, None] * inv[None, :]
    cos = jnp.cos(ang)[None, :, None, :]
    sin = jnp.sin(ang)[None, :, None, :]
    t1, t2 = t[..., : d // 2], t[..., d // 2:]
    return jnp.concatenate([t1 * cos - t2 * sin, t1 * sin + t2 * cos], axis=-1)


def causal_mask():
    return jnp.tril(jnp.ones((CHUNK, CHUNK), dtype=bool))


def mlstm_chunkwise(q, k, v, i_pre, f_pre):
    bsz, seq, nh, dk = q.shape
    dv = v.shape[-1]
    nc = seq // CHUNK
    q = (q * dk ** -0.5).reshape(bsz, nc, CHUNK, nh, dk)
    k = k.reshape(bsz, nc, CHUNK, nh, dk)
    v = v.reshape(bsz, nc, CHUNK, nh, dv)
    ig = i_pre.reshape(bsz, nc, CHUNK, nh)
    b = jnp.cumsum(jax.nn.log_sigmoid(f_pre).reshape(bsz, nc, CHUNK, nh), axis=2)
    g = b[:, :, -1]
    mask = causal_mask()[None, None, :, :, None]
    log_d = jnp.where(mask, b[:, :, :, None, :] - b[:, :, None, :, :] + ig[:, :, None, :, :], -jnp.inf)
    log_w = g[:, :, None, :] - b + ig
    a = jnp.max(log_w, axis=2)
    w = jnp.exp(log_w - a[:, :, None, :])
    d_c = jnp.einsum('bclh,bclhk,bclhv->bchkv', w, k, v)
    d_n = jnp.einsum('bclh,bclhk->bchk', w, k)

    def step(carry, inp):
        c, n, m = carry
        g_c, a_c, dc, dn = inp
        m_new = jnp.maximum(g_c + m, a_c)
        s_old = jnp.exp(g_c + m - m_new)
        s_new = jnp.exp(a_c - m_new)
        c_new = s_old[..., None, None] * c + s_new[..., None, None] * dc
        n_new = s_old[..., None] * n + s_new[..., None] * dn
        return (c_new, n_new, m_new), (c, n, m)

    init = (jnp.zeros((bsz, nh, dk, dv), F32), jnp.zeros((bsz, nh, dk), F32), jnp.zeros((bsz, nh), F32))
    xs = (jnp.moveaxis(g, 1, 0), jnp.moveaxis(a, 1, 0), jnp.moveaxis(d_c, 1, 0), jnp.moveaxis(d_n, 1, 0))
    _, (c_prev, n_prev, m_prev) = lax.scan(step, init, xs)
    c_prev = jnp.moveaxis(c_prev, 0, 1)
    n_prev = jnp.moveaxis(n_prev, 0, 1)
    m_prev = jnp.moveaxis(m_prev, 0, 1)

    m_inter = b + m_prev[:, :, None, :]
    m_t = jnp.maximum(m_inter, jnp.max(log_d, axis=3))
    s = jnp.einsum('bcthk,bcshk->bctsh', q, k) * jnp.exp(log_d - m_t[:, :, :, None, :])
    scale_inter = jnp.exp(m_inter - m_t)
    num = jnp.einsum('bctsh,bcshv->bcthv', s, v) + scale_inter[..., None] * jnp.einsum('bcthk,bchkv->bcthv', q, c_prev)
    den = jnp.sum(s, axis=3) + scale_inter * jnp.einsum('bcthk,bchk->bcth', q, n_prev)
    h = num / jnp.maximum(jnp.abs(den), jnp.exp(-m_t))[..., None]
    return h.reshape(bsz, seq, nh, dv)


def retention_chunkwise(q, k, v):
    bsz, seq, nh, dk = q.shape
    dv = v.shape[-1]
    nc = seq // CHUNK
    log_gamma = jnp.log(1.0 - 2.0 ** (-5.0 - jnp.arange(nh, dtype=F32)))
    q = (q * dk ** -0.5).reshape(bsz, nc, CHUNK, nh, dk)
    k = k.reshape(bsz, nc, CHUNK, nh, dk)
    v = v.reshape(bsz, nc, CHUNK, nh, dv)
    idx = jnp.arange(CHUNK, dtype=F32)
    rel = idx[:, None] - idx[None, :]
    decay = jnp.where(causal_mask()[:, :, None], jnp.exp(jnp.maximum(rel, 0.0)[:, :, None] * log_gamma), 0.0)
    s = jnp.einsum('bcthk,bcshk->bctsh', q, k) * decay
    intra = jnp.einsum('bctsh,bcshv->bcthv', s, v)
    w_k = jnp.exp((CHUNK - 1.0 - idx)[:, None] * log_gamma)
    d_r = jnp.einsum('lh,bclhk,bclhv->bchkv', w_k, k, v)
    chunk_decay = jnp.exp(CHUNK * log_gamma)

    def step(r, dr):
        return chunk_decay[:, None, None] * r + dr, r

    _, r_prev = lax.scan(step, jnp.zeros((bsz, nh, dk, dv), F32), jnp.moveaxis(d_r, 1, 0))
    r_prev = jnp.moveaxis(r_prev, 0, 1)
    w_q = jnp.exp((idx + 1.0)[:, None] * log_gamma)
    inter = w_q[None, None, :, :, None] * jnp.einsum('bcthk,bchkv->bcthv', q, r_prev)
    return (intra + inter).reshape(bsz, seq, nh, dv)


def hybrid_mixer(u, w_in, i_bias, f_bias, m_norm_w, r_norm_w, w_out):
    bsz, seq, _ = u.shape
    proj = (u @ w_in).astype(F32)
    mq, mk, mv, mi, mf, mo, rq, rk, rv, rg = jnp.split(proj, HYB_SPLITS, axis=-1)
    h_m = mlstm_chunkwise(mq.reshape(bsz, seq, MLSTM_HEADS, MLSTM_QK),
                          mk.reshape(bsz, seq, MLSTM_HEADS, MLSTM_QK),
                          mv.reshape(bsz, seq, MLSTM_HEADS, MLSTM_V),
                          mi + i_bias.astype(F32), mf + f_bias.astype(F32))
    out_m = jax.nn.sigmoid(mo) * headnorm(h_m, m_norm_w)
    h_r = retention_chunkwise(rotary(rq.reshape(bsz, seq, RET_HEADS, RET_QK)),
                              rotary(rk.reshape(bsz, seq, RET_HEADS, RET_QK)),
                              rv.reshape(bsz, seq, RET_HEADS, RET_V))
    out_r = jax.nn.silu(rg) * headnorm(h_r, r_norm_w)
    cat = jnp.concatenate([out_m, out_r], axis=-1).astype(u.dtype)
    return cat @ w_out


def causal_depthwise_conv(x, w, b):
    out = lax.conv_general_dilated(x, w.astype(x.dtype)[:, None, :], window_strides=(1,),
                                   padding=[(SSD_CONV - 1, 0)],
                                   dimension_numbers=('NWC', 'WIO', 'NWC'),
                                   feature_group_count=x.shape[-1])
    return out + b.astype(x.dtype)


def ssd_chunked(x, dt, a, bm, cm):
    bsz, seq, nh, hp = x.shape
    ng, ns = bm.shape[2], bm.shape[3]
    rep = nh // ng
    nc = seq // CHUNK
    x = x.reshape(bsz, nc, CHUNK, ng, rep, hp)
    dt = dt.reshape(bsz, nc, CHUNK, ng, rep)
    bm = bm.reshape(bsz, nc, CHUNK, ng, ns)
    cm = cm.reshape(bsz, nc, CHUNK, ng, ns)
    cs = jnp.cumsum(dt * a.reshape(ng, rep), axis=2)
    mask = causal_mask()[None, None, :, :, None, None]
    seg = jnp.exp(jnp.where(mask, cs[:, :, :, None] - cs[:, :, None, :], -jnp.inf))
    cb = jnp.einsum('bctgn,bcsgn->bctsg', cm, bm)
    m = cb[..., None] * seg * dt[:, :, None]
    y_intra = jnp.einsum('bctsgr,bcsgrp->bctgrp', m, x)
    to_end = jnp.exp(cs[:, :, -1:] - cs) * dt
    states = jnp.einsum('bclgn,bclgr,bclgrp->bcgrpn', bm, to_end, x)
    chunk_decay = jnp.exp(cs[:, :, -1])

    def step(h, inp):
        dec, st = inp
        return dec[..., None, None] * h + st, h

    _, h_prev = lax.scan(step, jnp.zeros((bsz, ng, rep, hp, ns), F32),
                         (jnp.moveaxis(chunk_decay, 1, 0), jnp.moveaxis(states, 1, 0)))
    h_prev = jnp.moveaxis(h_prev, 0, 1)
    y_inter = jnp.einsum('bctgn,bcgrpn->bctgrp', cm, h_prev) * jnp.exp(cs)[..., None]
    return (y_intra + y_inter).reshape(bsz, seq, nh, hp)


def ssd_mixer(u, w_in, conv_w, conv_b, dt_bias, a_log, d_skip, norm_w, w_out):
    bsz, seq, _ = u.shape
    zxbcdt = u @ w_in
    z, xbc, dt = jnp.split(zxbcdt, (SSD_INNER, SSD_INNER + SSD_CONV_DIM), axis=-1)
    xbc = jax.nn.silu(causal_depthwise_conv(xbc, conv_w, conv_b).astype(F32))
    xs, bm, cm = jnp.split(xbc, (SSD_INNER, SSD_INNER + SSD_GROUPS * SSD_STATE), axis=-1)
    xs = xs.reshape(bsz, seq, SSD_HEADS, SSD_HEADDIM)
    bm = bm.reshape(bsz, seq, SSD_GROUPS, SSD_STATE)
    cm = cm.reshape(bsz, seq, SSD_GROUPS, SSD_STATE)
    dt = jax.nn.softplus(dt.astype(F32) + dt_bias.astype(F32))
    a = -jnp.exp(a_log.astype(F32))
    y = ssd_chunked(xs, dt, a, bm, cm) + d_skip.astype(F32)[:, None] * xs
    yg = (y.reshape(bsz, seq, SSD_INNER) * jax.nn.silu(z.astype(F32))).reshape(bsz, seq, SSD_GROUPS, -1)
    yg = yg * lax.rsqrt(jnp.mean(yg * yg, axis=-1, keepdims=True) + 1e-5)
    yg = yg.reshape(bsz, seq, SSD_INNER) * norm_w.astype(F32)
    return yg.astype(u.dtype) @ w_out


def squared_relu_mlp(u, w1, w2):
    h = jax.nn.relu(u @ w1)
    return (h * h) @ w2


def setup_inputs(seed: int = 0) -> dict:
    key = jax.random.key(seed)
    ks = jax.random.split(key, 20)
    n_even = (DEPTH + 1) // 2
    n_odd = DEPTH // 2
    nrm = jax.random.normal
    dt = jnp.exp(jax.random.uniform(ks[11], (n_odd, SSD_HEADS)) * (math.log(0.1) - math.log(0.001)) + math.log(0.001))
    dt = jnp.maximum(dt, 1e-4)
    return {
        "x": nrm(ks[0], (BATCH, SEQ, D_MODEL), F32),
        "norm_mix_w": 1.0 + 0.02 * nrm(ks[1], (DEPTH, D_MODEL), F32),
        "norm_mlp_w": 1.0 + 0.02 * nrm(ks[2], (DEPTH, D_MODEL), F32),
        "hyb_w_in": nrm(ks[3], (n_even, D_MODEL, HYB_PROJ), F32) * D_MODEL ** -0.5,
        "mlstm_i_bias": 0.1 * nrm(ks[4], (n_even, MLSTM_HEADS), F32),
        "mlstm_f_bias": jnp.linspace(3.0, 6.0, MLSTM_HEADS, dtype=F32)[None, :] + 0.1 * nrm(ks[5], (n_even, MLSTM_HEADS), F32),
        "mlstm_norm_w": 1.0 + 0.02 * nrm(ks[6], (n_even, MLSTM_INNER), F32),
        "ret_norm_w": 1.0 + 0.02 * nrm(ks[7], (n_even, RET_INNER), F32),
        "hyb_w_out": nrm(ks[8], (n_even, HYB_INNER, D_MODEL), F32) * HYB_INNER ** -0.5,
        "ssd_w_in": nrm(ks[9], (n_odd, D_MODEL, SSD_PROJ), F32) * D_MODEL ** -0.5,
        "ssd_conv_w": nrm(ks[10], (n_odd, SSD_CONV, SSD_CONV_DIM), F32) * SSD_CONV ** -0.5,
        "ssd_conv_b": 0.02 * nrm(ks[12], (n_odd, SSD_CONV_DIM), F32),
        "ssd_dt_bias": dt + jnp.log(-jnp.expm1(-dt)),
        "ssd_a_log": jnp.log(jax.random.uniform(ks[13], (n_odd, SSD_HEADS), F32, 1.0, 16.0)),
        "ssd_d": 1.0 + 0.1 * nrm(ks[14], (n_odd, SSD_HEADS), F32),
        "ssd_norm_w": 1.0 + 0.02 * nrm(ks[15], (n_odd, SSD_INNER), F32),
        "ssd_w_out": nrm(ks[16], (n_odd, SSD_INNER, D_MODEL), F32) * SSD_INNER ** -0.5,
        "mlp_w1": nrm(ks[17], (DEPTH, D_MODEL, D_FF), F32) * D_MODEL ** -0.5,
        "mlp_w2": nrm(ks[18], (DEPTH, D_FF, D_MODEL), F32) * D_FF ** -0.5,
        "final_norm_w": 1.0 + 0.02 * nrm(ks[19], (D_MODEL,), F32),
    }


def reference(x, norm_mix_w, norm_mlp_w, hyb_w_in, mlstm_i_bias, mlstm_f_bias, mlstm_norm_w, ret_norm_w,
              hyb_w_out, ssd_w_in, ssd_conv_w, ssd_conv_b, ssd_dt_bias, ssd_a_log, ssd_d, ssd_norm_w,
              ssd_w_out, mlp_w1, mlp_w2, final_norm_w):
    h = x
    for layer in range(DEPTH):
        u = rmsnorm(h, norm_mix_w[layer])
        j = layer // 2
        if layer % 2 == 0:
            mix = hybrid_mixer(u, hyb_w_in[j], mlstm_i_bias[j], mlstm_f_bias[j], mlstm_norm_w[j],
                               ret_norm_w[j], hyb_w_out[j])
        else:
            mix = ssd_mixer(u, ssd_w_in[j], ssd_conv_w[j], ssd_conv_b[j], ssd_dt_bias[j], ssd_a_log[j],
                            ssd_d[j], ssd_norm_w[j], ssd_w_out[j])
        h = h + mix.astype(h.dtype)
        h = h + squared_relu_mlp(rmsnorm(h, norm_mlp_w[layer]), mlp_w1[layer], mlp_w2[layer]).astype(h.dtype)
    return rmsnorm(h, final_norm_w)
```

```python
import functools

import jax
import jax.numpy as jnp
from jax import lax
from jax.experimental import pallas as pl
from jax.experimental.pallas import tpu as pltpu

F32 = jnp.float32
BF16 = jnp.bfloat16
HIGHEST = lax.Precision.HIGHEST

CHUNK = 128
LANES = 128
SUBLANES = 8
VMEM_LIMIT_BYTES = 56 * 1024 * 1024
ROPE_BASE = 10000.0

MLSTM_HEADS = 4
RET_HEADS = 4
QK_ALL = 256
HEAD_V = 128
SSD_GROUPS = 8
SSD_REP = 4
SSD_P = 64
SSD_N = 128
SSD_GW = SSD_REP * SSD_P
SSD_INNER = SSD_GROUPS * SSD_GW
SSD_CONV = 4

H_MQ, H_MK, H_MV, H_MO = 0, 256, 512, 1024
H_RQ, H_RK, H_RV, H_RG = 1536, 1792, 2048, 2560
H_GATE = 3072
H_PROJ = H_GATE + LANES
S_Z, S_X, S_B, S_C, S_DT = 0, 2048, 4096, 5120, 6144
S_PROJ = S_DT + LANES
S_CONV_DIM = S_DT - S_X

TM_PROJ = 256
TM_MLP = 512
TS_CORE = 256
FF_CHUNK = 1024


def _log1p_exp_neg_abs(x):
    return jnp.log1p(jnp.exp(-jnp.abs(x)))


def _softplus(x):
    return jnp.maximum(x, 0.0) + _log1p_exp_neg_abs(x)


def _log_sigmoid(x):
    return jnp.minimum(x, 0.0) - _log1p_exp_neg_abs(x)


def _silu(x):
    return x * jax.nn.sigmoid(x)


def _rmsnorm(x, w_row, eps):
    return x * lax.rsqrt(jnp.mean(x * x, axis=-1, keepdims=True) + eps) * w_row


def _headnorm(hv, w_row):
    mu = jnp.mean(hv, axis=-1, keepdims=True)
    d = hv - mu
    var = jnp.mean(d * d, axis=-1, keepdims=True)
    return d * lax.rsqrt(var + 1e-5) * w_row


def _bdot(a, b):
    return jnp.dot(a.astype(BF16), b.astype(BF16), preferred_element_type=F32)


def _compiler_params(semantics):
    return pltpu.CompilerParams(dimension_semantics=semantics, vmem_limit_bytes=VMEM_LIMIT_BYTES)


def _resident(shape):
    zeros = (0,) * len(shape)
    return pl.BlockSpec(shape, lambda *_: zeros, pipeline_mode=pl.Buffered(1))


def _norm_matmul_kernel(x_ref, nw_ref, w_ref, o_ref):
    xn = _rmsnorm(x_ref[...], nw_ref[...], 1e-6).astype(BF16)
    o_ref[...] = jnp.dot(xn, w_ref[...], preferred_element_type=F32)


def _norm_matmul(x2d, nw_row, w_bf16):
    t, d = x2d.shape
    n = w_bf16.shape[1]
    return pl.pallas_call(
        _norm_matmul_kernel,
        out_shape=jax.ShapeDtypeStruct((t, n), F32),
        grid=(t // TM_PROJ,),
        in_specs=[pl.BlockSpec((TM_PROJ, d), lambda i: (i, 0)), _resident((1, d)), _resident((d, n))],
        out_specs=pl.BlockSpec((TM_PROJ, n), lambda i: (i, 0)),
        compiler_params=_compiler_params(("parallel",)),
        name="norm_matmul",
    )(x2d, nw_row, w_bf16)


def _mlp_kernel(x_ref, nw_ref, w1_ref, w2_ref, fnw_ref, o_ref, *, final):
    x = x_ref[...]
    xn = _rmsnorm(x, nw_ref[...], 1e-6).astype(BF16)
    acc = x
    for f in range(w1_ref.shape[1] // FF_CHUNK):
        cols = slice(f * FF_CHUNK, (f + 1) * FF_CHUNK)
        hid = jnp.maximum(jnp.dot(xn, w1_ref[:, cols], preferred_element_type=F32), 0.0)
        acc = acc + jnp.dot((hid * hid).astype(BF16), w2_ref[cols, :], preferred_element_type=F32)
    if final:
        acc = _rmsnorm(acc, fnw_ref[...], 1e-6)
    o_ref[...] = acc


def _mlp(x2d, nw_row, w1, w2, fnw_row, final):
    t, d = x2d.shape
    dff = w1.shape[1]
    return pl.pallas_call(
        functools.partial(_mlp_kernel, final=final),
        out_shape=jax.ShapeDtypeStruct((t, d), F32),
        grid=(t // TM_MLP,),
        in_specs=[pl.BlockSpec((TM_MLP, d), lambda i: (i, 0)), _resident((1, d)), _resident((d, dff)),
                  _resident((dff, d)), _resident((1, d))],
        out_specs=pl.BlockSpec((TM_MLP, d), lambda i: (i, 0)),
        compiler_params=_compiler_params(("parallel",)),
        name="mlp",
    )(x2d, nw_row, w1, w2, fnw_row)


def _hyb_core_kernel(proj_ref, h_ref, cos_ref, sin_ref, gbias_ref, nw_ref, dec_ref, rtab_ref, wout_ref, o_ref,
                     cst_ref, mst_ref, rst_ref, ycat_ref):
    @pl.when(pl.program_id(1) == 0)
    def _init():
        cst_ref[...] = jnp.zeros_like(cst_ref)
        mst_ref[...] = jnp.zeros_like(mst_ref)
        rst_ref[...] = jnp.zeros_like(rst_ref)

    row = lax.broadcasted_iota(jnp.int32, (CHUNK, CHUNK), 0)
    col = lax.broadcasted_iota(jnp.int32, (CHUNK, CHUNK), 1)
    causal = row >= col
    tri = causal.astype(F32)
    lane_qk = lax.broadcasted_iota(jnp.int32, (1, QK_ALL), 1)
    lane0 = lax.broadcasted_iota(jnp.int32, (CHUNK, LANES), 1) == 0
    neg_inf = jnp.float32(-jnp.inf)
    n_chunks = proj_ref.shape[0] // CHUNK

    def chunk_body(c, carry):
        rows = pl.ds(pl.multiple_of(c * CHUNK, CHUNK), CHUNK)

        gb = proj_ref[rows, H_GATE:H_GATE + LANES] + gbias_ref[...]
        bc = jnp.dot(tri, _log_sigmoid(gb), precision=HIGHEST, preferred_element_type=F32)
        bct = bc.T
        gbt = gb.T
        mq = (proj_ref[rows, H_MQ:H_MQ + QK_ALL] * 0.125).astype(BF16)
        mkt = proj_ref[rows, H_MK:H_MK + QK_ALL].T.astype(BF16)
        for h in range(MLSTM_HEADS):
            qh = jnp.where(lane_qk // 64 == h, mq, jnp.zeros_like(mq))
            b_col, b_row = bc[:, 4 + h:5 + h], bct[4 + h:5 + h, :]
            i_col, i_row = gb[:, h:h + 1], gbt[h:h + 1, :]
            g = bc[CHUNK - 1:CHUNK, 4 + h:5 + h]
            m_prev = mst_ref[h:h + 1, 0:1]
            c_prev = cst_ref[h]
            v = proj_ref[rows, H_MV + h * HEAD_V:H_MV + (h + 1) * HEAD_V]

            log_d = jnp.where(causal, b_col - b_row + i_row, neg_inf)
            m_inter = b_col + m_prev
            m_t = jnp.maximum(m_inter, jnp.max(log_d, axis=1, keepdims=True))
            s = jnp.dot(qh, mkt, preferred_element_type=F32) * jnp.exp(log_d - m_t)
            scale_inter = jnp.exp(m_inter - m_t)
            inter = jnp.dot(qh, c_prev.astype(BF16), preferred_element_type=F32)
            num = _bdot(s, v) + scale_inter * inter[:, :HEAD_V]
            den = jnp.sum(s, axis=1, keepdims=True) + scale_inter * inter[:, HEAD_V:HEAD_V + 1]
            hm = num / jnp.maximum(jnp.abs(den), jnp.exp(-m_t))
            gate = jax.nn.sigmoid(proj_ref[rows, H_MO + h * HEAD_V:H_MO + (h + 1) * HEAD_V])
            ycat_ref[rows, h * HEAD_V:(h + 1) * HEAD_V] = (
                gate * _headnorm(hm, nw_ref[:, h * HEAD_V:(h + 1) * HEAD_V])).astype(BF16)

            log_w = g - b_col + i_col
            a = jnp.max(log_w, axis=0, keepdims=True)
            w = jnp.exp(log_w - a)
            vext = jnp.concatenate([v * w, jnp.where(lane0, w, 0.0)], axis=1)
            m_new = jnp.maximum(g + m_prev, a)
            cst_ref[h] = jnp.exp(g + m_prev - m_new) * c_prev + jnp.exp(a - m_new) * _bdot(mkt, vext)
            mst_ref[h:h + 1, :] = jnp.broadcast_to(m_new, (1, LANES))

        cs, sn = cos_ref[rows, :], sin_ref[rows, :]
        q1, q2 = proj_ref[rows, H_RQ:H_RQ + LANES], proj_ref[rows, H_RQ + LANES:H_RQ + 2 * LANES]
        k1, k2 = proj_ref[rows, H_RK:H_RK + LANES], proj_ref[rows, H_RK + LANES:H_RK + 2 * LANES]
        rq = (jnp.concatenate([q1 * cs - q2 * sn, q1 * sn + q2 * cs], axis=1) * 0.125).astype(BF16)
        rkt = jnp.concatenate([k1 * cs - k2 * sn, k1 * sn + k2 * cs], axis=1).T.astype(BF16)
        for h in range(RET_HEADS):
            qh = jnp.where((lane_qk % LANES) // 32 == h, rq, jnp.zeros_like(rq))
            v = proj_ref[rows, H_RV + h * HEAD_V:H_RV + (h + 1) * HEAD_V]
            r_prev = rst_ref[h]
            w_q, w_k = rtab_ref[:, h:h + 1], rtab_ref[:, 4 + h:5 + h]
            chunk_decay = rtab_ref[0:1, 8 + h:9 + h]
            s = jnp.dot(qh, rkt, preferred_element_type=F32) * dec_ref[h]
            out = _bdot(s, v) + w_q * jnp.dot(qh, r_prev.astype(BF16), preferred_element_type=F32)
            rst_ref[h] = chunk_decay * r_prev + _bdot(rkt, v * w_k)
            gate = _silu(proj_ref[rows, H_RG + h * HEAD_V:H_RG + (h + 1) * HEAD_V])
            c0 = (MLSTM_HEADS + h) * HEAD_V
            ycat_ref[rows, c0:c0 + HEAD_V] = (gate * _headnorm(out, nw_ref[:, c0:c0 + HEAD_V])).astype(BF16)
        return carry

    lax.fori_loop(0, n_chunks, chunk_body, 0)
    o_ref[...] = h_ref[...] + jnp.dot(ycat_ref[...], wout_ref[...], preferred_element_type=F32)


def _hyb_core(proj, h, cos_t, sin_t, gbias, nw_row, dec, rtab, wout):
    b, s, d = h.shape
    ts = TS_CORE
    inner = wout.shape[0]
    return pl.pallas_call(
        _hyb_core_kernel,
        out_shape=jax.ShapeDtypeStruct((b, s, d), F32),
        grid=(b, s // ts),
        in_specs=[
            pl.BlockSpec((None, ts, H_PROJ), lambda bi, i: (bi, i, 0)),
            pl.BlockSpec((None, ts, d), lambda bi, i: (bi, i, 0)),
            pl.BlockSpec((ts, LANES), lambda bi, i: (i, 0)),
            pl.BlockSpec((ts, LANES), lambda bi, i: (i, 0)),
            _resident((1, LANES)),
            _resident((1, inner)),
            _resident((RET_HEADS, CHUNK, CHUNK)),
            _resident((CHUNK, LANES)),
            _resident((inner, d)),
        ],
        out_specs=pl.BlockSpec((None, ts, d), lambda bi, i: (bi, i, 0)),
        scratch_shapes=[
            pltpu.VMEM((MLSTM_HEADS, QK_ALL, 2 * HEAD_V), F32),
            pltpu.VMEM((SUBLANES, LANES), F32),
            pltpu.VMEM((RET_HEADS, QK_ALL, HEAD_V), F32),
            pltpu.VMEM((ts, inner), BF16),
        ],
        compiler_params=_compiler_params(("parallel", "arbitrary")),
        name="hybrid_core",
    )(proj, h, cos_t, sin_t, gbias, nw_row, dec, rtab, wout)


CONV_COLS = 512


def _ssd_core_kernel(proj_ref, h_ref, cw_ref, cb_ref, dtb_ref, alog_ref, dsk_ref, nw_ref, wout_ref, o_ref,
                     halo_ref, st_ref, xc_ref, ycat_ref):
    @pl.when(pl.program_id(1) == 0)
    def _init():
        halo_ref[...] = jnp.zeros_like(halo_ref)
        st_ref[...] = jnp.zeros_like(st_ref)

    row = lax.broadcasted_iota(jnp.int32, (CHUNK, CHUNK), 0)
    col = lax.broadcasted_iota(jnp.int32, (CHUNK, CHUNK), 1)
    causal = row >= col
    tri = causal.astype(F32)
    row8 = lax.broadcasted_iota(jnp.int32, (SUBLANES, CONV_COLS), 0)
    lane_g = lax.broadcasted_iota(jnp.int32, (1, SSD_GW), 1)
    neg_inf = jnp.float32(-jnp.inf)
    a_row = -jnp.exp(alog_ref[...])
    n_chunks = proj_ref.shape[0] // CHUNK

    def chunk_body(c, carry):
        rows = pl.ds(pl.multiple_of(c * CHUNK, CHUNK), CHUNK)

        for cb in range(S_CONV_DIM // CONV_COLS):
            cc = slice(cb * CONV_COLS, (cb + 1) * CONV_COLS)
            raw = proj_ref[rows, S_X + cb * CONV_COLS:S_X + (cb + 1) * CONV_COLS]
            prev = halo_ref[:, cc]
            acc = raw * cw_ref[SSD_CONV - 1:SSD_CONV, cc] + cb_ref[:, cc]
            for j in range(1, SSD_CONV):
                rolled = pltpu.roll(raw, j, 0)
                first = jnp.where(row8 < j, pltpu.roll(prev, j, 0), rolled[:SUBLANES])
                shifted = jnp.concatenate([first, rolled[SUBLANES:]], axis=0)
                acc = acc + shifted * cw_ref[SSD_CONV - 1 - j:SSD_CONV - j, cc]
            halo_ref[:, cc] = raw[CHUNK - SUBLANES:]
            xc_ref[:, cc] = _silu(acc)

        dtp = _softplus(proj_ref[rows, S_DT:S_DT + LANES] + dtb_ref[...])
        cs = jnp.dot(tri, dtp * a_row, precision=HIGHEST, preferred_element_type=F32)
        cst = cs.T
        dtt = dtp.T
        ecs = jnp.exp(cs)
        toendt = jnp.exp(cst[:, CHUNK - 1:CHUNK] - cst) * dtt
        dec_last = jnp.exp(cs[CHUNK - 1:CHUNK, :])

        for g in range(SSD_GROUPS):
            gcols = slice(g * SSD_GW, (g + 1) * SSD_GW)
            bg = xc_ref[:, S_B - S_X + g * SSD_N:S_B - S_X + (g + 1) * SSD_N]
            cg = xc_ref[:, S_C - S_X + g * SSD_N:S_C - S_X + (g + 1) * SSD_N]
            xg = xc_ref[:, gcols]
            cbm = lax.dot_general(cg.astype(BF16), bg.astype(BF16), (((1,), (1,)), ((), ())),
                                  preferred_element_type=F32)
            bgt = bg.T
            xb = xg.astype(BF16)
            st_prev = st_ref[g]
            hb = st_prev.astype(BF16)
            yg = dsk_ref[:, gcols] * xg
            newst = jnp.zeros((SSD_N, SSD_GW), F32)
            decay_row = jnp.zeros((1, SSD_GW), F32)
            for j in range(SSD_REP):
                hh = g * SSD_REP + j
                jmask = lane_g // SSD_P == j
                xm = jnp.where(jmask, xb, jnp.zeros_like(xb))
                hm = jnp.where(jmask, hb, jnp.zeros_like(hb))
                seg = jnp.exp(jnp.where(causal, cs[:, hh:hh + 1] - cst[hh:hh + 1, :], neg_inf))
                m = cbm * seg * dtt[hh:hh + 1, :]
                lhs = jnp.concatenate([m, cg * ecs[:, hh:hh + 1]], axis=1).astype(BF16)
                rhs = jnp.concatenate([xm, hm], axis=0)
                yg = yg + jnp.dot(lhs, rhs, preferred_element_type=F32)
                newst = newst + jnp.dot((bgt * toendt[hh:hh + 1, :]).astype(BF16), xm,
                                        preferred_element_type=F32)
                decay_row = jnp.where(jmask, dec_last[:, hh:hh + 1], decay_row)
            st_ref[g] = st_prev * decay_row + newst
            yz = yg * _silu(proj_ref[rows, S_Z + g * SSD_GW:S_Z + (g + 1) * SSD_GW])
            ycat_ref[rows, gcols] = _rmsnorm(yz, nw_ref[:, gcols], 1e-5).astype(BF16)
        return carry

    lax.fori_loop(0, n_chunks, chunk_body, 0)
    o_ref[...] = h_ref[...] + jnp.dot(ycat_ref[...], wout_ref[...], preferred_element_type=F32)


def _ssd_core(proj, h, cw, cb_row, dtb, alog, dsk, nw_row, wout):
    b, s, d = h.shape
    ts = TS_CORE
    return pl.pallas_call(
        _ssd_core_kernel,
        out_shape=jax.ShapeDtypeStruct((b, s, d), F32),
        grid=(b, s // ts),
        in_specs=[
            pl.BlockSpec((None, ts, S_PROJ), lambda bi, i: (bi, i, 0)),
            pl.BlockSpec((None, ts, d), lambda bi, i: (bi, i, 0)),
            _resident((SSD_CONV, S_CONV_DIM)),
            _resident((1, S_CONV_DIM)),
            _resident((1, LANES)),
            _resident((1, LANES)),
            _resident((1, SSD_INNER)),
            _resident((1, SSD_INNER)),
            _resident((SSD_INNER, d)),
        ],
        out_specs=pl.BlockSpec((None, ts, d), lambda bi, i: (bi, i, 0)),
        scratch_shapes=[
            pltpu.VMEM((SUBLANES, S_CONV_DIM), F32),
            pltpu.VMEM((SSD_GROUPS, SSD_N, SSD_GW), F32),
            pltpu.VMEM((CHUNK, S_CONV_DIM), F32),
            pltpu.VMEM((ts, SSD_INNER), BF16),
        ],
        compiler_params=_compiler_params(("parallel", "arbitrary")),
        name="ssd_core",
    )(proj, h, cw, cb_row, dtb, alog, dsk, nw_row, wout)


def _pad_lanes(row, width=LANES):
    return jnp.pad(row, ((0, 0), (0, width - row.shape[1])))


def _hybrid_layer(h, nw, w_in, i_bias, f_bias, m_norm_w, r_norm_w, w_out, cos_t, sin_t, dec, rtab):
    b, s, d = h.shape
    mq, mk, mv, mi, mf, mo, rq, rk, rv, rg = jnp.split(
        w_in, (256, 512, 1024, 1028, 1032, 1544, 1800, 2056, 2568), axis=1)
    perm = jnp.concatenate([
        (jnp.arange(LANES) // 32) * 64 + jnp.arange(LANES) % 32,
        (jnp.arange(LANES) // 32) * 64 + 32 + jnp.arange(LANES) % 32])
    gates = jnp.pad(jnp.concatenate([mi, mf], axis=1), ((0, 0), (0, LANES - 2 * MLSTM_HEADS)))
    w_cat = jnp.concatenate([mq, mk, mv, mo, rq[:, perm], rk[:, perm], rv, rg, gates], axis=1).astype(BF16)
    gbias = _pad_lanes(jnp.concatenate([i_bias, f_bias])[None, :])
    proj = _norm_matmul(h.reshape(b * s, d), nw[None, :], w_cat).reshape(b, s, H_PROJ)
    nw_row = jnp.concatenate([m_norm_w, r_norm_w])[None, :]
    return _hyb_core(proj, h, cos_t, sin_t, gbias, nw_row, dec, rtab, w_out.astype(BF16))


def _ssd_layer(h, nw, w_in, conv_w, conv_b, dt_bias, a_log, d_skip, norm_w, w_out):
    b, s, d = h.shape
    w_cat = jnp.pad(w_in, ((0, 0), (0, S_PROJ - w_in.shape[1]))).astype(BF16)
    proj = _norm_matmul(h.reshape(b * s, d), nw[None, :], w_cat).reshape(b, s, S_PROJ)
    dsk = jnp.repeat(d_skip, SSD_P)[None, :]
    return _ssd_core(proj, h, conv_w, conv_b[None, :], _pad_lanes(dt_bias[None, :]), _pad_lanes(a_log[None, :]),
                     dsk, norm_w[None, :], w_out.astype(BF16))


def _position_tables(seq):
    dk = 64
    inv = ROPE_BASE ** (-jnp.arange(0, dk, 2, dtype=F32) / dk)
    ang = jnp.arange(seq, dtype=F32)[:, None] * inv[None, :]
    cos_t = jnp.tile(jnp.cos(ang), (1, RET_HEADS))
    sin_t = jnp.tile(jnp.sin(ang), (1, RET_HEADS))
    log_gamma = jnp.log(1.0 - 2.0 ** (-5.0 - jnp.arange(RET_HEADS, dtype=F32)))
    idx = jnp.arange(CHUNK, dtype=F32)
    rel = idx[:, None] - idx[None, :]
    dec = jnp.where((rel >= 0)[None], jnp.exp(jnp.maximum(rel, 0.0)[None] * log_gamma[:, None, None]), 0.0)
    w_q = jnp.exp((idx + 1.0)[:, None] * log_gamma)
    w_k = jnp.exp((CHUNK - 1.0 - idx)[:, None] * log_gamma)
    chunk_decay = jnp.broadcast_to(jnp.exp(CHUNK * log_gamma)[None, :], (CHUNK, RET_HEADS))
    rtab = _pad_lanes(jnp.concatenate([w_q, w_k, chunk_decay], axis=1))
    return cos_t, sin_t, dec, rtab


@jax.jit
def kernel(x, norm_mix_w, norm_mlp_w, hyb_w_in, mlstm_i_bias, mlstm_f_bias, mlstm_norm_w, ret_norm_w, hyb_w_out,
           ssd_w_in, ssd_conv_w, ssd_conv_b, ssd_dt_bias, ssd_a_log, ssd_d, ssd_norm_w, ssd_w_out, mlp_w1, mlp_w2,
           final_norm_w):
    b, s, d = x.shape
    depth = norm_mix_w.shape[0]
    cos_t, sin_t, dec, rtab = _position_tables(s)
    h = x
    for layer in range(depth):
        j = layer // 2
        if layer % 2 == 0:
            h = _hybrid_layer(h, norm_mix_w[layer], hyb_w_in[j], mlstm_i_bias[j], mlstm_f_bias[j], mlstm_norm_w[j],
                              ret_norm_w[j], hyb_w_out[j], cos_t, sin_t, dec, rtab)
        else:
            h = _ssd_layer(h, norm_mix_w[layer], ssd_w_in[j], ssd_conv_w[j], ssd_conv_b[j], ssd_dt_bias[j],
                           ssd_a_log[j], ssd_d[j], ssd_norm_w[j], ssd_w_out[j])
        h = _mlp(h.reshape(b * s, d), norm_mlp_w[layer][None, :], mlp_w1[layer].astype(BF16),
                 mlp_w2[layer].astype(BF16), final_norm_w[None, :], layer == depth - 1).reshape(b, s, d)
    return h
```

```python
import functools

import jax
import jax.numpy as jnp
from jax import lax
from jax.experimental import pallas as pl
from jax.experimental.pallas import tpu as pltpu

F32 = jnp.float32
BF16 = jnp.bfloat16
HIGHEST = lax.Precision.HIGHEST

CHUNK = 128
LANES = 128
SUBLANES = 8
VMEM_LIMIT_BYTES = 56 * 1024 * 1024
ROPE_BASE = 10000.0

MLSTM_HEADS = 4
RET_HEADS = 4
QK_ALL = 256
HEAD_V = 128
SSD_GROUPS = 8
SSD_REP = 4
SSD_P = 64
SSD_N = 128
SSD_GW = SSD_REP * SSD_P
SSD_INNER = SSD_GROUPS * SSD_GW
SSD_CONV = 4

H_MQ, H_MK, H_MV, H_MO = 0, 256, 512, 1024
H_RQ, H_RK, H_RV, H_RG = 1536, 1792, 2048, 2560
H_GATE = 3072
H_PROJ = H_GATE + LANES
S_Z, S_X, S_B, S_C, S_DT = 0, 2048, 4096, 5120, 6144
S_PROJ = S_DT + LANES
S_CONV_DIM = S_DT - S_X

TM_PROJ = 256
TM_MLP = 512
TS_CORE = 256
FF_CHUNK = 1024


def _log1p_exp_neg_abs(x):
    return jnp.log1p(jnp.exp(-jnp.abs(x)))


def _softplus(x):
    return jnp.maximum(x, 0.0) + _log1p_exp_neg_abs(x)


def _log_sigmoid(x):
    return jnp.minimum(x, 0.0) - _log1p_exp_neg_abs(x)


def _sigmoid(x):
    return 0.5 * jnp.tanh(0.5 * x) + 0.5


def _silu(x):
    half = 0.5 * x
    return half * jnp.tanh(half) + half


def _rmsnorm(x, w_row, eps):
    return x * lax.rsqrt(jnp.mean(x * x, axis=-1, keepdims=True) + eps) * w_row


def _headnorm(hv, w_row):
    mu = jnp.mean(hv, axis=-1, keepdims=True)
    d = hv - mu
    var = jnp.mean(d * d, axis=-1, keepdims=True)
    return d * lax.rsqrt(var + 1e-5) * w_row


def _token_of_row(r):
    return (r % SUBLANES) * (CHUNK // SUBLANES) + r // SUBLANES


def _causal_mask():
    row = lax.broadcasted_iota(jnp.int32, (CHUNK, CHUNK), 0)
    col = lax.broadcasted_iota(jnp.int32, (CHUNK, CHUNK), 1)
    return _token_of_row(row) >= _token_of_row(col)


def _bdot(a, b):
    return jnp.dot(a.astype(BF16), b.astype(BF16), preferred_element_type=F32)


def _compiler_params(semantics):
    return pltpu.CompilerParams(dimension_semantics=semantics, vmem_limit_bytes=VMEM_LIMIT_BYTES)


def _resident(shape):
    zeros = (0,) * len(shape)
    return pl.BlockSpec(shape, lambda *_: zeros, pipeline_mode=pl.Buffered(1))


def _norm_matmul_kernel(x_ref, nw_ref, w_ref, o_ref):
    xn = _rmsnorm(x_ref[...], nw_ref[...], 1e-6).astype(BF16)
    o_ref[...] = jnp.dot(xn, w_ref[...], preferred_element_type=F32)


def _norm_matmul(x2d, nw_row, w_bf16):
    t, d = x2d.shape
    n = w_bf16.shape[1]
    return pl.pallas_call(
        _norm_matmul_kernel,
        out_shape=jax.ShapeDtypeStruct((t, n), F32),
        grid=(t // TM_PROJ,),
        in_specs=[pl.BlockSpec((TM_PROJ, d), lambda i: (i, 0)), _resident((1, d)), _resident((d, n))],
        out_specs=pl.BlockSpec((TM_PROJ, n), lambda i: (i, 0)),
        compiler_params=_compiler_params(("parallel",)),
        name="norm_matmul",
    )(x2d, nw_row, w_bf16)


def _mlp_kernel(x_ref, nw_ref, w1_ref, w2_ref, fnw_ref, o_ref, *, final):
    x = x_ref[...]
    xn = _rmsnorm(x, nw_ref[...], 1e-6).astype(BF16)
    acc = x
    for f in range(w1_ref.shape[1] // FF_CHUNK):
        cols = slice(f * FF_CHUNK, (f + 1) * FF_CHUNK)
        hid = jnp.maximum(jnp.dot(xn, w1_ref[:, cols], preferred_element_type=F32), 0.0)
        acc = acc + jnp.dot((hid * hid).astype(BF16), w2_ref[cols, :], preferred_element_type=F32)
    if final:
        acc = _rmsnorm(acc, fnw_ref[...], 1e-6)
    o_ref[...] = acc


def _mlp(x2d, nw_row, w1, w2, fnw_row, final):
    t, d = x2d.shape
    dff = w1.shape[1]
    return pl.pallas_call(
        functools.partial(_mlp_kernel, final=final),
        out_shape=jax.ShapeDtypeStruct((t, d), F32),
        grid=(t // TM_MLP,),
        in_specs=[pl.BlockSpec((TM_MLP, d), lambda i: (i, 0)), _resident((1, d)), _resident((d, dff)),
                  _resident((dff, d)), _resident((1, d))],
        out_specs=pl.BlockSpec((TM_MLP, d), lambda i: (i, 0)),
        compiler_params=_compiler_params(("parallel",)),
        name="mlp",
    )(x2d, nw_row, w1, w2, fnw_row)


def _hyb_core_kernel(proj_ref, h_ref, cos_ref, sin_ref, gbias_ref, nw_ref, dec_ref, rtab_ref, wout_ref, o_ref,
                     cst_ref, mst_ref, rst_ref, ycat_ref):
    @pl.when(pl.program_id(1) == 0)
    def _init():
        cst_ref[...] = jnp.zeros_like(cst_ref)
        mst_ref[...] = jnp.zeros_like(mst_ref)
        rst_ref[...] = jnp.zeros_like(rst_ref)

    causal = _causal_mask()
    tri = causal.astype(F32)
    lane_qk = lax.broadcasted_iota(jnp.int32, (1, QK_ALL), 1)
    lane0 = lax.broadcasted_iota(jnp.int32, (CHUNK, LANES), 1) == 0
    neg_inf = jnp.float32(-jnp.inf)
    n_chunks = proj_ref.shape[0] // CHUNK

    def chunk_body(c, carry):
        rows = pl.ds(pl.multiple_of(c * CHUNK, CHUNK), CHUNK)

        gb = proj_ref[rows, H_GATE:H_GATE + LANES] + gbias_ref[...]
        bc = jnp.dot(tri, _log_sigmoid(gb), precision=HIGHEST, preferred_element_type=F32)
        bct = bc.T
        gbt = gb.T
        mq = (proj_ref[rows, H_MQ:H_MQ + QK_ALL] * 0.125).astype(BF16)
        mkt = proj_ref[rows, H_MK:H_MK + QK_ALL].T.astype(BF16)
        for h in range(MLSTM_HEADS):
            qh = jnp.where(lane_qk // 64 == h, mq, jnp.zeros_like(mq))
            b_col, b_row = bc[:, 4 + h:5 + h], bct[4 + h:5 + h, :]
            i_col, i_row = gb[:, h:h + 1], gbt[h:h + 1, :]
            g = bc[CHUNK - 1:CHUNK, 4 + h:5 + h]
            m_prev = mst_ref[h:h + 1, 0:1]
            c_prev = cst_ref[h]
            v = proj_ref[rows, H_MV + h * HEAD_V:H_MV + (h + 1) * HEAD_V]

            log_d = jnp.where(causal, b_col - b_row + i_row, neg_inf)
            m_inter = b_col + m_prev
            m_t = jnp.maximum(m_inter, jnp.max(log_d, axis=1, keepdims=True))
            s = jnp.dot(qh, mkt, preferred_element_type=F32) * jnp.exp(log_d - m_t)
            scale_inter = jnp.exp(m_inter - m_t)
            inter = jnp.dot(qh, c_prev.astype(BF16), preferred_element_type=F32)
            num = _bdot(s, v) + scale_inter * inter[:, :HEAD_V]
            den = jnp.sum(s, axis=1, keepdims=True) + scale_inter * inter[:, HEAD_V:HEAD_V + 1]
            hm = num / jnp.maximum(jnp.abs(den), jnp.exp(-m_t))
            gate = _sigmoid(proj_ref[rows, H_MO + h * HEAD_V:H_MO + (h + 1) * HEAD_V])
            ycat_ref[rows, h * HEAD_V:(h + 1) * HEAD_V] = (
                gate * _headnorm(hm, nw_ref[:, h * HEAD_V:(h + 1) * HEAD_V])).astype(BF16)

            log_w = g - b_col + i_col
            a = jnp.max(log_w, axis=0, keepdims=True)
            w = jnp.exp(log_w - a)
            vext = jnp.concatenate([v * w, jnp.where(lane0, w, 0.0)], axis=1)
            m_new = jnp.maximum(g + m_prev, a)
            cst_ref[h] = jnp.exp(g + m_prev - m_new) * c_prev + jnp.exp(a - m_new) * _bdot(mkt, vext)
            mst_ref[h:h + 1, :] = jnp.broadcast_to(m_new, (1, LANES))

        cs, sn = cos_ref[rows, :], sin_ref[rows, :]
        q1, q2 = proj_ref[rows, H_RQ:H_RQ + LANES], proj_ref[rows, H_RQ + LANES:H_RQ + 2 * LANES]
        k1, k2 = proj_ref[rows, H_RK:H_RK + LANES], proj_ref[rows, H_RK + LANES:H_RK + 2 * LANES]
        rq = (jnp.concatenate([q1 * cs - q2 * sn, q1 * sn + q2 * cs], axis=1) * 0.125).astype(BF16)
        rkt = jnp.concatenate([k1 * cs - k2 * sn, k1 * sn + k2 * cs], axis=1).T.astype(BF16)
        for h in range(RET_HEADS):
            qh = jnp.where((lane_qk % LANES) // 32 == h, rq, jnp.zeros_like(rq))
            v = proj_ref[rows, H_RV + h * HEAD_V:H_RV + (h + 1) * HEAD_V]
            r_prev = rst_ref[h]
            w_q, w_k = rtab_ref[:, h:h + 1], rtab_ref[:, 4 + h:5 + h]
            chunk_decay = rtab_ref[0:1, 8 + h:9 + h]
            s = jnp.dot(qh, rkt, preferred_element_type=F32) * dec_ref[h]
            out = _bdot(s, v) + w_q * jnp.dot(qh, r_prev.astype(BF16), preferred_element_type=F32)
            rst_ref[h] = chunk_decay * r_prev + _bdot(rkt, v * w_k)
            gate = _silu(proj_ref[rows, H_RG + h * HEAD_V:H_RG + (h + 1) * HEAD_V])
            c0 = (MLSTM_HEADS + h) * HEAD_V
            ycat_ref[rows, c0:c0 + HEAD_V] = (gate * _headnorm(out, nw_ref[:, c0:c0 + HEAD_V])).astype(BF16)
        return carry

    lax.fori_loop(0, n_chunks, chunk_body, 0)
    o_ref[...] = h_ref[...] + jnp.dot(ycat_ref[...], wout_ref[...], preferred_element_type=F32)


def _hyb_core(proj, h, cos_t, sin_t, gbias, nw_row, dec, rtab, wout):
    b, s, d = h.shape
    ts = TS_CORE
    inner = wout.shape[0]
    return pl.pallas_call(
        _hyb_core_kernel,
        out_shape=jax.ShapeDtypeStruct((b, s, d), F32),
        grid=(b, s // ts),
        in_specs=[
            pl.BlockSpec((None, ts, H_PROJ), lambda bi, i: (bi, i, 0)),
            pl.BlockSpec((None, ts, d), lambda bi, i: (bi, i, 0)),
            pl.BlockSpec((ts, LANES), lambda bi, i: (i, 0)),
            pl.BlockSpec((ts, LANES), lambda bi, i: (i, 0)),
            _resident((1, LANES)),
            _resident((1, inner)),
            _resident((RET_HEADS, CHUNK, CHUNK)),
            _resident((CHUNK, LANES)),
            _resident((inner, d)),
        ],
        out_specs=pl.BlockSpec((None, ts, d), lambda bi, i: (bi, i, 0)),
        scratch_shapes=[
            pltpu.VMEM((MLSTM_HEADS, QK_ALL, 2 * HEAD_V), F32),
            pltpu.VMEM((SUBLANES, LANES), F32),
            pltpu.VMEM((RET_HEADS, QK_ALL, HEAD_V), F32),
            pltpu.VMEM((ts, inner), BF16),
        ],
        compiler_params=_compiler_params(("parallel", "arbitrary")),
        name="hybrid_core",
    )(proj, h, cos_t, sin_t, gbias, nw_row, dec, rtab, wout)


CONV_COLS = 512
HALO_SLABS = SSD_CONV - 1


def _ssd_core_kernel(proj_ref, h_ref, cw_ref, cb_ref, dtb_ref, alog_ref, dsk_ref, nw_ref, wout_ref, o_ref,
                     halo_ref, st_ref, xc_ref, ycat_ref):
    @pl.when(pl.program_id(1) == 0)
    def _init():
        halo_ref[...] = jnp.zeros_like(halo_ref)
        st_ref[...] = jnp.zeros_like(st_ref)

    causal = _causal_mask()
    tri = causal.astype(F32)
    lane_g = lax.broadcasted_iota(jnp.int32, (1, SSD_GW), 1)
    last_sublane = lax.broadcasted_iota(jnp.int32, (SUBLANES, CONV_COLS), 0) == SUBLANES - 1
    neg_inf = jnp.float32(-jnp.inf)
    a_row = -jnp.exp(alog_ref[...])
    n_chunks = proj_ref.shape[0] // CHUNK

    def chunk_body(c, carry):
        rows = pl.ds(pl.multiple_of(c * CHUNK, CHUNK), CHUNK)

        for cb in range(S_CONV_DIM // CONV_COLS):
            cc = slice(cb * CONV_COLS, (cb + 1) * CONV_COLS)
            raw = proj_ref[rows, S_X + cb * CONV_COLS:S_X + (cb + 1) * CONV_COLS]
            wrapped = []
            for k in range(HALO_SLABS):
                cur = raw[CHUNK - (HALO_SLABS - k) * SUBLANES:CHUNK - (HALO_SLABS - k - 1) * SUBLANES]
                prv = halo_ref[k * SUBLANES:(k + 1) * SUBLANES, cc]
                wrapped.append(pltpu.roll(jnp.where(last_sublane, prv, cur), 1, 0))
            acc = raw * cw_ref[SSD_CONV - 1:SSD_CONV, cc] + cb_ref[:, cc]
            for j in range(1, SSD_CONV):
                shifted = jnp.concatenate(wrapped[HALO_SLABS - j:] + [raw[:CHUNK - j * SUBLANES]], axis=0)
                acc = acc + shifted * cw_ref[SSD_CONV - 1 - j:SSD_CONV - j, cc]
            halo_ref[:, cc] = raw[CHUNK - HALO_SLABS * SUBLANES:]
            xc_ref[:, cc] = _silu(acc)

        dtp = _softplus(proj_ref[rows, S_DT:S_DT + LANES] + dtb_ref[...])
        cs = jnp.dot(tri, dtp * a_row, precision=HIGHEST, preferred_element_type=F32)
        cst = cs.T
        dtt = dtp.T
        ecs = jnp.exp(cs)
        toendt = jnp.exp(cst[:, CHUNK - 1:CHUNK] - cst) * dtt
        dec_last = jnp.exp(cs[CHUNK - 1:CHUNK, :])

        for g in range(SSD_GROUPS):
            gcols = slice(g * SSD_GW, (g + 1) * SSD_GW)
            bg = xc_ref[:, S_B - S_X + g * SSD_N:S_B - S_X + (g + 1) * SSD_N]
            cg = xc_ref[:, S_C - S_X + g * SSD_N:S_C - S_X + (g + 1) * SSD_N]
            xg = xc_ref[:, gcols]
            cbm = lax.dot_general(cg.astype(BF16), bg.astype(BF16), (((1,), (1,)), ((), ())),
                                  preferred_element_type=F32)
            bgt = bg.T
            xb = xg.astype(BF16)
            st_prev = st_ref[g]
            hb = st_prev.astype(BF16)
            yg = dsk_ref[:, gcols] * xg
            newst = jnp.zeros((SSD_N, SSD_GW), F32)
            decay_row = jnp.zeros((1, SSD_GW), F32)
            for j in range(SSD_REP):
                hh = g * SSD_REP + j
                jmask = lane_g // SSD_P == j
                xm = jnp.where(jmask, xb, jnp.zeros_like(xb))
                hm = jnp.where(jmask, hb, jnp.zeros_like(hb))
                seg = jnp.exp(jnp.where(causal, cs[:, hh:hh + 1] - cst[hh:hh + 1, :], neg_inf))
                m = cbm * seg * dtt[hh:hh + 1, :]
                lhs = jnp.concatenate([m, cg * ecs[:, hh:hh + 1]], axis=1).astype(BF16)
                rhs = jnp.concatenate([xm, hm], axis=0)
                yg = yg + jnp.dot(lhs, rhs, preferred_element_type=F32)
                newst = newst + jnp.dot((bgt * toendt[hh:hh + 1, :]).astype(BF16), xm,
                                        preferred_element_type=F32)
                decay_row = jnp.where(jmask, dec_last[:, hh:hh + 1], decay_row)
            st_ref[g] = st_prev * decay_row + newst
            yz = yg * _silu(proj_ref[rows, S_Z + g * SSD_GW:S_Z + (g + 1) * SSD_GW])
            ycat_ref[rows, gcols] = _rmsnorm(yz, nw_ref[:, gcols], 1e-5).astype(BF16)
        return carry

    lax.fori_loop(0, n_chunks, chunk_body, 0)
    o_ref[...] = h_ref[...] + jnp.dot(ycat_ref[...], wout_ref[...], preferred_element_type=F32)


def _ssd_core(proj, h, cw, cb_row, dtb, alog, dsk, nw_row, wout):
    b, s, d = h.shape
    ts = TS_CORE
    return pl.pallas_call(
        _ssd_core_kernel,
        out_shape=jax.ShapeDtypeStruct((b, s, d), F32),
        grid=(b, s // ts),
        in_specs=[
            pl.BlockSpec((None, ts, S_PROJ), lambda bi, i: (bi, i, 0)),
            pl.BlockSpec((None, ts, d), lambda bi, i: (bi, i, 0)),
            _resident((SSD_CONV, S_CONV_DIM)),
            _resident((1, S_CONV_DIM)),
            _resident((1, LANES)),
            _resident((1, LANES)),
            _resident((1, SSD_INNER)),
            _resident((1, SSD_INNER)),
            _resident((SSD_INNER, d)),
        ],
        out_specs=pl.BlockSpec((None, ts, d), lambda bi, i: (bi, i, 0)),
        scratch_shapes=[
            pltpu.VMEM((HALO_SLABS * SUBLANES, S_CONV_DIM), F32),
            pltpu.VMEM((SSD_GROUPS, SSD_N, SSD_GW), F32),
            pltpu.VMEM((CHUNK, S_CONV_DIM), F32),
            pltpu.VMEM((ts, SSD_INNER), BF16),
        ],
        compiler_params=_compiler_params(("parallel", "arbitrary")),
        name="ssd_core",
    )(proj, h, cw, cb_row, dtb, alog, dsk, nw_row, wout)


def _pad_lanes(row, width=LANES):
    return jnp.pad(row, ((0, 0), (0, width - row.shape[1])))


def _hybrid_layer(h, nw, w_in, i_bias, f_bias, m_norm_w, r_norm_w, w_out, cos_t, sin_t, dec, rtab):
    b, s, d = h.shape
    mq, mk, mv, mi, mf, mo, rq, rk, rv, rg = jnp.split(
        w_in, (256, 512, 1024, 1028, 1032, 1544, 1800, 2056, 2568), axis=1)
    perm = jnp.concatenate([
        (jnp.arange(LANES) // 32) * 64 + jnp.arange(LANES) % 32,
        (jnp.arange(LANES) // 32) * 64 + 32 + jnp.arange(LANES) % 32])
    gates = jnp.pad(jnp.concatenate([mi, mf], axis=1), ((0, 0), (0, LANES - 2 * MLSTM_HEADS)))
    w_cat = jnp.concatenate([mq, mk, mv, mo, rq[:, perm], rk[:, perm], rv, rg, gates], axis=1).astype(BF16)
    gbias = _pad_lanes(jnp.concatenate([i_bias, f_bias])[None, :])
    proj = _norm_matmul(h.reshape(b * s, d), nw[None, :], w_cat).reshape(b, s, H_PROJ)
    nw_row = jnp.concatenate([m_norm_w, r_norm_w])[None, :]
    return _hyb_core(proj, h, cos_t, sin_t, gbias, nw_row, dec, rtab, w_out.astype(BF16))


def _ssd_layer(h, nw, w_in, conv_w, conv_b, dt_bias, a_log, d_skip, norm_w, w_out):
    b, s, d = h.shape
    w_cat = jnp.pad(w_in, ((0, 0), (0, S_PROJ - w_in.shape[1]))).astype(BF16)
    proj = _norm_matmul(h.reshape(b * s, d), nw[None, :], w_cat).reshape(b, s, S_PROJ)
    dsk = jnp.repeat(d_skip, SSD_P)[None, :]
    return _ssd_core(proj, h, conv_w, conv_b[None, :], _pad_lanes(dt_bias[None, :]), _pad_lanes(a_log[None, :]),
                     dsk, norm_w[None, :], w_out.astype(BF16))


def _position_tables(seq):
    dk = 64
    inv = ROPE_BASE ** (-jnp.arange(0, dk, 2, dtype=F32) / dk)
    ang = jnp.arange(seq, dtype=F32)[:, None] * inv[None, :]
    cos_t = jnp.tile(jnp.cos(ang), (1, RET_HEADS))
    sin_t = jnp.tile(jnp.sin(ang), (1, RET_HEADS))
    log_gamma = jnp.log(1.0 - 2.0 ** (-5.0 - jnp.arange(RET_HEADS, dtype=F32)))
    idx = jnp.arange(CHUNK, dtype=F32)
    rel = idx[:, None] - idx[None, :]
    dec = jnp.where((rel >= 0)[None], jnp.exp(jnp.maximum(rel, 0.0)[None] * log_gamma[:, None, None]), 0.0)
    w_q = jnp.exp((idx + 1.0)[:, None] * log_gamma)
    w_k = jnp.exp((CHUNK - 1.0 - idx)[:, None] * log_gamma)
    chunk_decay = jnp.broadcast_to(jnp.exp(CHUNK * log_gamma)[None, :], (CHUNK, RET_HEADS))
    rtab = _pad_lanes(jnp.concatenate([w_q, w_k, chunk_decay], axis=1))
    tok = _token_of_row(jnp.arange(CHUNK))
    return _interleave_tokens(cos_t, 0), _interleave_tokens(sin_t, 0), dec[:, tok][:, :, tok], rtab[tok]


def _interleave_tokens(a, axis):
    shape = a.shape
    split = shape[:axis] + (shape[axis] // CHUNK, SUBLANES, CHUNK // SUBLANES) + shape[axis + 1:]
    return jnp.swapaxes(a.reshape(split), axis + 1, axis + 2).reshape(shape)


def _deinterleave_tokens(a, axis):
    shape = a.shape
    split = shape[:axis] + (shape[axis] // CHUNK, CHUNK // SUBLANES, SUBLANES) + shape[axis + 1:]
    return jnp.swapaxes(a.reshape(split), axis + 1, axis + 2).reshape(shape)


@jax.jit
def kernel(x, norm_mix_w, norm_mlp_w, hyb_w_in, mlstm_i_bias, mlstm_f_bias, mlstm_norm_w, ret_norm_w, hyb_w_out,
           ssd_w_in, ssd_conv_w, ssd_conv_b, ssd_dt_bias, ssd_a_log, ssd_d, ssd_norm_w, ssd_w_out, mlp_w1, mlp_w2,
           final_norm_w):
    b, s, d = x.shape
    depth = norm_mix_w.shape[0]
    cos_t, sin_t, dec, rtab = _position_tables(s)
    h = _interleave_tokens(x, 1)
    for layer in range(depth):
        j = layer // 2
        if layer % 2 == 0:
            h = _hybrid_layer(h, norm_mix_w[layer], hyb_w_in[j], mlstm_i_bias[j], mlstm_f_bias[j], mlstm_norm_w[j],
                              ret_norm_w[j], hyb_w_out[j], cos_t, sin_t, dec, rtab)
        else:
            h = _ssd_layer(h, norm_mix_w[layer], ssd_w_in[j], ssd_conv_w[j], ssd_conv_b[j], ssd_dt_bias[j],
                           ssd_a_log[j], ssd_d[j], ssd_norm_w[j], ssd_w_out[j])
        h = _mlp(h.reshape(b * s, d), norm_mlp_w[layer][None, :], mlp_w1[layer].astype(BF16),
                 mlp_w2[layer].astype(BF16), final_norm_w[None, :], layer == depth - 1).reshape(b, s, d)
    return _deinterleave_tokens(h, 1)
```

```python
import functools

import jax
import jax.numpy as jnp
from jax import lax
from jax.experimental import pallas as pl
from jax.experimental.pallas import tpu as pltpu

F32 = jnp.float32
BF16 = jnp.bfloat16

CHUNK = 128
LANES = 128
SUBLANES = 8
VMEM_LIMIT_BYTES = 56 * 1024 * 1024
ROPE_BASE = 10000.0

MLSTM_HEADS = 4
RET_HEADS = 4
HEAD_QK = 64
QK_ALL = 256
HEAD_V = 128
SSD_GROUPS = 8
SSD_REP = 4
SSD_P = 64
SSD_N = 128
SSD_HEADS = SSD_GROUPS * SSD_REP
SSD_GW = SSD_REP * SSD_P
SSD_INNER = SSD_GROUPS * SSD_GW
SSD_CONV = 4

H_MQ, H_MK, H_MV, H_MO = 0, 256, 512, 1024
H_RQ, H_RK, H_RV, H_RG = 1536, 1792, 2048, 2560
H_GI, H_GF = 3072, 3200
H_PROJ = H_GF + LANES
S_Z, S_X, S_B, S_C, S_DT = 0, 2048, 4096, 5120, 6144
S_PROJ = S_DT + LANES
S_CONV_DIM = S_DT - S_X

TM_PROJ = 512
TM_SSD_PROJ = 256
CONV_COLS = 512
HALO_SLABS = SSD_CONV - 1
TM_MLP = 512
TS_CORE = 512
FF_CHUNK = 1024


def _log1p_exp_neg_abs(x):
    return jnp.log1p(jnp.exp(-jnp.abs(x)))


def _softplus(x):
    return jnp.maximum(x, 0.0) + _log1p_exp_neg_abs(x)


def _log_sigmoid(x):
    return jnp.minimum(x, 0.0) - _log1p_exp_neg_abs(x)


def _sigmoid(x):
    return 0.5 * jnp.tanh(0.5 * x) + 0.5


def _silu(x):
    half = 0.5 * x
    return half * jnp.tanh(half) + half


def _rmsnorm(x, w_row, eps):
    return x * lax.rsqrt(jnp.mean(x * x, axis=-1, keepdims=True) + eps) * w_row


def _token_of_row(r):
    return (r % SUBLANES) * (CHUNK // SUBLANES) + r // SUBLANES


def _causal_mask():
    row = lax.broadcasted_iota(jnp.int32, (CHUNK, CHUNK), 0)
    col = lax.broadcasted_iota(jnp.int32, (CHUNK, CHUNK), 1)
    return _token_of_row(row) >= _token_of_row(col)


def _split3(x):
    hi = x.astype(BF16)
    r1 = x - hi.astype(F32)
    mid = r1.astype(BF16)
    lo = (r1 - mid.astype(F32)).astype(BF16)
    return hi, mid, lo


def _chunk_cumsum(tri3, x):
    return jnp.dot(tri3, jnp.concatenate(_split3(x), axis=0), preferred_element_type=F32)


def _lane_replicate(x, sel3):
    return jnp.dot(jnp.concatenate(_split3(x), axis=1), sel3, preferred_element_type=F32)


def _compiler_params(semantics):
    return pltpu.CompilerParams(dimension_semantics=semantics, vmem_limit_bytes=VMEM_LIMIT_BYTES)


def _resident(shape):
    zeros = (0,) * len(shape)
    return pl.BlockSpec(shape, lambda *_: zeros, pipeline_mode=pl.Buffered(1))


def _norm_matmul_kernel(x_ref, nw_ref, w_ref, o_ref):
    xn = _rmsnorm(x_ref[...], nw_ref[...], 1e-6).astype(BF16)
    o_ref[...] = jnp.dot(xn, w_ref[...], preferred_element_type=F32)


def _norm_matmul(x2d, nw_row, w_bf16):
    t, d = x2d.shape
    n = w_bf16.shape[1]
    return pl.pallas_call(
        _norm_matmul_kernel,
        out_shape=jax.ShapeDtypeStruct((t, n), F32),
        grid=(t // TM_PROJ,),
        in_specs=[pl.BlockSpec((TM_PROJ, d), lambda i: (i, 0)), _resident((1, d)), _resident((d, n))],
        out_specs=pl.BlockSpec((TM_PROJ, n), lambda i: (i, 0)),
        compiler_params=_compiler_params(("parallel",)),
        name="norm_matmul",
    )(x2d, nw_row, w_bf16)


def _causal_conv_silu(raw, prev_tail, cw_ref, cb_ref, cc, last_sublane):
    wrapped = []
    for k in range(HALO_SLABS):
        cur = raw[CHUNK - (HALO_SLABS - k) * SUBLANES:CHUNK - (HALO_SLABS - k - 1) * SUBLANES]
        prv = prev_tail[k * SUBLANES:(k + 1) * SUBLANES]
        wrapped.append(pltpu.roll(jnp.where(last_sublane, prv, cur), 1, 0))
    acc = raw * cw_ref[SSD_CONV - 1:SSD_CONV, cc] + cb_ref[:, cc]
    for j in range(1, SSD_CONV):
        shifted = jnp.concatenate(wrapped[HALO_SLABS - j:] + [raw[:CHUNK - j * SUBLANES]], axis=0)
        acc = acc + shifted * cw_ref[SSD_CONV - 1 - j:SSD_CONV - j, cc]
    return _silu(acc)


def _ssd_proj_kernel(x_ref, nw_ref, w_ref, cw_ref, cb_ref, o_ref, halo_ref):
    @pl.when(pl.program_id(1) == 0)
    def _init():
        halo_ref[...] = jnp.zeros_like(halo_ref)

    xn = _rmsnorm(x_ref[...], nw_ref[...], 1e-6).astype(BF16)
    last_sublane = lax.broadcasted_iota(jnp.int32, (SUBLANES, CONV_COLS), 0) == SUBLANES - 1
    n_chunks = x_ref.shape[0] // CHUNK
    tail = slice(CHUNK - HALO_SLABS * SUBLANES, CHUNK)

    for cb in range(S_X // CONV_COLS):
        cols = slice(cb * CONV_COLS, (cb + 1) * CONV_COLS)
        o_ref[:, cols] = _silu(jnp.dot(xn, w_ref[:, cols], preferred_element_type=F32))
    o_ref[:, S_DT:] = jnp.dot(xn, w_ref[:, S_DT:], preferred_element_type=F32)
    for cb in range(S_CONV_DIM // CONV_COLS):
        cc = slice(cb * CONV_COLS, (cb + 1) * CONV_COLS)
        pc = slice(S_X + cb * CONV_COLS, S_X + (cb + 1) * CONV_COLS)
        raw_all = jnp.dot(xn, w_ref[:, pc], preferred_element_type=F32)
        prev_tail = halo_ref[:, cc]
        for c in range(n_chunks):
            raw = raw_all[c * CHUNK:(c + 1) * CHUNK]
            o_ref[c * CHUNK:(c + 1) * CHUNK, pc] = _causal_conv_silu(raw, prev_tail, cw_ref, cb_ref, cc, last_sublane)
            prev_tail = raw[tail]
        halo_ref[:, cc] = prev_tail


def _ssd_proj(h, nw_row, w_bf16, cw, cb_row):
    b, s, d = h.shape
    tm = TM_SSD_PROJ
    return pl.pallas_call(
        _ssd_proj_kernel,
        out_shape=jax.ShapeDtypeStruct((b, s, S_PROJ), F32),
        grid=(b, s // tm),
        in_specs=[pl.BlockSpec((None, tm, d), lambda bi, i: (bi, i, 0)), _resident((1, d)), _resident((d, S_PROJ)),
                  _resident((SSD_CONV, S_CONV_DIM)), _resident((1, S_CONV_DIM))],
        out_specs=pl.BlockSpec((None, tm, S_PROJ), lambda bi, i: (bi, i, 0)),
        scratch_shapes=[pltpu.VMEM((HALO_SLABS * SUBLANES, S_CONV_DIM), F32)],
        compiler_params=_compiler_params(("parallel", "arbitrary")),
        name="ssd_proj",
    )(h, nw_row, w_bf16, cw, cb_row)


def _mlp_kernel(x_ref, nw_ref, w1_ref, w2_ref, fnw_ref, o_ref, *, final):
    x = x_ref[...]
    xn = _rmsnorm(x, nw_ref[...], 1e-6).astype(BF16)
    acc = x
    for f in range(w1_ref.shape[1] // FF_CHUNK):
        cols = slice(f * FF_CHUNK, (f + 1) * FF_CHUNK)
        hid = jnp.maximum(jnp.dot(xn, w1_ref[:, cols], preferred_element_type=F32), 0.0)
        acc = acc + jnp.dot((hid * hid).astype(BF16), w2_ref[cols, :], preferred_element_type=F32)
    if final:
        acc = _rmsnorm(acc, fnw_ref[...], 1e-6)
    o_ref[...] = acc


def _mlp(x2d, nw_row, w1, w2, fnw_row, final):
    t, d = x2d.shape
    dff = w1.shape[1]
    return pl.pallas_call(
        functools.partial(_mlp_kernel, final=final),
        out_shape=jax.ShapeDtypeStruct((t, d), F32),
        grid=(t // TM_MLP,),
        in_specs=[pl.BlockSpec((TM_MLP, d), lambda i: (i, 0)), _resident((1, d)), _resident((d, dff)),
                  _resident((dff, d)), _resident((1, d))],
        out_specs=pl.BlockSpec((TM_MLP, d), lambda i: (i, 0)),
        compiler_params=_compiler_params(("parallel",)),
        name="mlp",
    )(x2d, nw_row, w1, w2, fnw_row)


def _shift_tokens(x, d, fill):
    n_slab = CHUNK // SUBLANES
    sub = lax.broadcasted_iota(jnp.int32, (SUBLANES, x.shape[1]), 0)
    slabs = [x[i * SUBLANES:(i + 1) * SUBLANES] for i in range(n_slab)]
    if d < n_slab:
        wrapped = [jnp.where(sub == 0, fill, pltpu.roll(slabs[n_slab - d + i], 1, 0)) for i in range(d)]
        return jnp.concatenate(wrapped + slabs[:n_slab - d], axis=0)
    k = d // n_slab
    return jnp.concatenate([jnp.where(sub < k, fill, pltpu.roll(sl, k, 0)) for sl in slabs], axis=0)


def _token_cummax(x):
    d = 1
    while d < CHUNK:
        x = jnp.maximum(x, _shift_tokens(x, d, -jnp.inf))
        d *= 2
    return x


def _split2(x):
    hi = x.astype(BF16)
    return hi, (x - hi.astype(F32)).astype(BF16)


def _row_mean_replicated(x, j2):
    return jnp.dot(jnp.concatenate(_split2(x), axis=1), j2, preferred_element_type=F32)


def _embed_rows(block, starts):
    piece = block.shape[0] // len(starts)
    parts, pos = [], 0
    for i, start in enumerate(starts):
        if start > pos:
            parts.append(jnp.zeros((start - pos, block.shape[1]), block.dtype))
        parts.append(block[i * piece:(i + 1) * piece])
        pos = start + piece
    if pos < QK_ALL:
        parts.append(jnp.zeros((QK_ALL - pos, block.shape[1]), block.dtype))
    return jnp.concatenate(parts, axis=0)


N_REP = 3 * MLSTM_HEADS


def _hyb_core_kernel(proj_ref, h_ref, cos_ref, sin_ref, gbias_ref, nw_ref, dec_ref, wq_ref, rrow_ref, sel_ref,
                     wout_ref, o_ref, cst_ref, mst_ref, rst_ref, rep_ref, x_ref, dn_ref, ycat_ref):
    @pl.when(pl.program_id(1) == 0)
    def _init():
        cst_ref[...] = jnp.zeros_like(cst_ref)
        mst_ref[...] = jnp.zeros_like(mst_ref)
        rst_ref[...] = jnp.zeros_like(rst_ref)

    causal = _causal_mask()
    tri3 = jnp.tile(causal.astype(BF16), (1, 3))
    lane_qk = lax.broadcasted_iota(jnp.int32, (1, QK_ALL), 1)
    lane = lax.broadcasted_iota(jnp.int32, (CHUNK, LANES), 1)
    ones_v = jnp.ones((CHUNK, HEAD_V), BF16)
    j2 = jnp.full((2 * HEAD_V, HEAD_V), 1.0 / HEAD_V, BF16)
    neg_inf = jnp.float32(-jnp.inf)
    n_chunks = proj_ref.shape[0] // CHUNK
    n_heads = MLSTM_HEADS + RET_HEADS
    half = HEAD_QK // 2

    def chunk_body(c, carry):
        rows = pl.ds(pl.multiple_of(c * CHUNK, CHUNK), CHUNK)

        gi = proj_ref[rows, H_GI:H_GI + LANES] + gbias_ref[0:1, :]
        gf = proj_ref[rows, H_GF:H_GF + LANES] + gbias_ref[1:2, :]
        bc = _chunk_cumsum(tri3, _log_sigmoid(gf))
        r = gi - bc
        m_prev = mst_ref[0:1, :]
        mu = jnp.maximum(m_prev, _token_cummax(r))
        g_row = bc[CHUNK - 1:CHUNK, :]
        m_new = jnp.maximum(g_row + m_prev, jnp.max(g_row + r, axis=0, keepdims=True))
        s_old = jnp.exp(g_row + m_prev - m_new)
        wt = jnp.exp(g_row + r - m_new).T
        rt = r.T
        mst_ref[0:1, :] = m_new
        packed = jnp.where(lane < MLSTM_HEADS, mu,
                           jnp.where(lane < 2 * MLSTM_HEADS, jnp.exp(m_prev - mu), jnp.exp(-bc - mu)))
        rep_ref[...] = _lane_replicate(packed, sel_ref[...])

        mq = (proj_ref[rows, H_MQ:H_MQ + QK_ALL] * 0.125).astype(BF16)
        mkt_f = proj_ref[rows, H_MK:H_MK + QK_ALL].T
        mkt = mkt_f.astype(BF16)
        for h in range(MLSTM_HEADS):
            hrows = slice(h * CHUNK, (h + 1) * CHUNK)
            mu_b = rep_ref[:, h * LANES:(h + 1) * LANES]
            si_b = rep_ref[:, (MLSTM_HEADS + h) * LANES:(MLSTM_HEADS + h + 1) * LANES]
            em_b = rep_ref[:, (2 * MLSTM_HEADS + h) * LANES:(2 * MLSTM_HEADS + h + 1) * LANES]
            qh = jnp.where(lane_qk // 64 == h, mq, jnp.zeros_like(mq))
            c_prev = cst_ref[h]
            vext = jnp.concatenate(
                [proj_ref[rows, H_MV + h * HEAD_V:H_MV + (h + 1) * HEAD_V].astype(BF16), ones_v], axis=1)

            dmat = jnp.exp(jnp.where(causal, rt[h:h + 1, :] - mu_b, neg_inf))
            s = jnp.dot(qh, mkt, preferred_element_type=F32) * dmat
            inter = jnp.dot(qh, _embed_rows(c_prev.astype(BF16), [h * HEAD_QK]),
                            preferred_element_type=F32)
            num_ext = (jnp.dot(s.astype(BF16), vext, preferred_element_type=F32)
                       + jnp.concatenate([si_b, si_b], axis=1) * inter)
            x_ref[hrows, :] = num_ext[:, :HEAD_V]
            dn_ref[hrows, :] = jnp.maximum(jnp.abs(num_ext[:, HEAD_V:]), em_b)
            kw = (mkt_f[h * HEAD_QK:(h + 1) * HEAD_QK] * wt[h:h + 1, :]).astype(BF16)
            cst_ref[h] = s_old[:, h:h + 1] * c_prev + jnp.dot(kw, vext, preferred_element_type=F32)

        cs, sn = cos_ref[rows, :], sin_ref[rows, :]
        q1, q2 = proj_ref[rows, H_RQ:H_RQ + LANES], proj_ref[rows, H_RQ + LANES:H_RQ + 2 * LANES]
        k1, k2 = proj_ref[rows, H_RK:H_RK + LANES], proj_ref[rows, H_RK + LANES:H_RK + 2 * LANES]
        rq = (jnp.concatenate([q1 * cs - q2 * sn, q1 * sn + q2 * cs], axis=1) * 0.125).astype(BF16)
        rkt_f = jnp.concatenate([k1 * cs - k2 * sn, k1 * sn + k2 * cs], axis=1).T
        rkt = rkt_f.astype(BF16)
        for h in range(RET_HEADS):
            hrows = slice((MLSTM_HEADS + h) * CHUNK, (MLSTM_HEADS + h + 1) * CHUNK)
            qh = jnp.where((lane_qk % LANES) // 32 == h, rq, jnp.zeros_like(rq))
            v = proj_ref[rows, H_RV + h * HEAD_V:H_RV + (h + 1) * HEAD_V].astype(BF16)
            r_prev = rst_ref[h]
            s = jnp.dot(qh, rkt, preferred_element_type=F32) * dec_ref[h]
            x_ref[hrows, :] = (jnp.dot(s.astype(BF16), v, preferred_element_type=F32)
                               + wq_ref[h] * jnp.dot(qh, _embed_rows(r_prev.astype(BF16), [h * half, LANES + h * half]),
                                                     preferred_element_type=F32))
            k_own = jnp.concatenate([rkt_f[h * half:(h + 1) * half], rkt_f[LANES + h * half:LANES + (h + 1) * half]],
                                    axis=0)
            kw = (k_own * rrow_ref[h:h + 1, :]).astype(BF16)
            rst_ref[h] = (rrow_ref[RET_HEADS + h:RET_HEADS + h + 1, 0:1] * r_prev
                          + jnp.dot(kw, v, preferred_element_type=F32))

        x = x_ref[...]
        d = x - _row_mean_replicated(x, j2)
        var = _row_mean_replicated(d * d, j2)
        dn = dn_ref[...]
        eps = jnp.concatenate([1e-5 * dn * dn, jnp.full((RET_HEADS * CHUNK, HEAD_V), 1e-5, F32)], axis=0)
        y = d * lax.rsqrt(var + eps)
        for h in range(n_heads):
            cols = slice(h * HEAD_V, (h + 1) * HEAD_V)
            if h < MLSTM_HEADS:
                gate = _sigmoid(proj_ref[rows, H_MO + h * HEAD_V:H_MO + (h + 1) * HEAD_V])
            else:
                hr = h - MLSTM_HEADS
                gate = _silu(proj_ref[rows, H_RG + hr * HEAD_V:H_RG + (hr + 1) * HEAD_V])
            ycat_ref[rows, cols] = (gate * (y[h * CHUNK:(h + 1) * CHUNK] * nw_ref[:, cols])).astype(BF16)
        return carry

    lax.fori_loop(0, n_chunks, chunk_body, 0)
    o_ref[...] = h_ref[...] + jnp.dot(ycat_ref[...], wout_ref[...], preferred_element_type=F32)


def _hyb_core(proj, h, cos_t, sin_t, gbias, nw_row, dec, wq_rep, rrow, wout):
    b, s, d = h.shape
    ts = TS_CORE
    inner = wout.shape[0]
    n_heads = MLSTM_HEADS + RET_HEADS
    return pl.pallas_call(
        _hyb_core_kernel,
        out_shape=jax.ShapeDtypeStruct((b, s, d), F32),
        grid=(b, s // ts),
        in_specs=[
            pl.BlockSpec((None, ts, H_PROJ), lambda bi, i: (bi, i, 0)),
            pl.BlockSpec((None, ts, d), lambda bi, i: (bi, i, 0)),
            pl.BlockSpec((ts, LANES), lambda bi, i: (i, 0)),
            pl.BlockSpec((ts, LANES), lambda bi, i: (i, 0)),
            _resident((SUBLANES, LANES)),
            _resident((1, inner)),
            _resident((RET_HEADS, CHUNK, CHUNK)),
            _resident((RET_HEADS, CHUNK, HEAD_V)),
            _resident((SUBLANES, LANES)),
            _resident((3 * LANES, N_REP * LANES)),
            _resident((inner, d)),
        ],
        out_specs=pl.BlockSpec((None, ts, d), lambda bi, i: (bi, i, 0)),
        scratch_shapes=[
            pltpu.VMEM((MLSTM_HEADS, HEAD_QK, 2 * HEAD_V), F32),
            pltpu.VMEM((SUBLANES, LANES), F32),
            pltpu.VMEM((RET_HEADS, HEAD_QK, HEAD_V), F32),
            pltpu.VMEM((CHUNK, N_REP * LANES), F32),
            pltpu.VMEM((n_heads * CHUNK, HEAD_V), F32),
            pltpu.VMEM((MLSTM_HEADS * CHUNK, HEAD_V), F32),
            pltpu.VMEM((ts, inner), BF16),
        ],
        compiler_params=_compiler_params(("parallel", "arbitrary")),
        name="hybrid_core",
    )(proj, h, cos_t, sin_t, gbias, nw_row, dec, wq_rep, rrow, _replicate_selector(N_REP), wout)


def _ssd_core_kernel(proj_ref, h_ref, dtb_ref, alog_ref, dsk_ref, nw_ref, sel_ref, wout_ref, o_ref,
                     st_ref, csb_ref, ycat_ref):
    @pl.when(pl.program_id(1) == 0)
    def _init():
        st_ref[...] = jnp.zeros_like(st_ref)

    causal = _causal_mask()
    tri3 = jnp.tile(causal.astype(BF16), (1, 3))
    lane_g = lax.broadcasted_iota(jnp.int32, (1, SSD_GW), 1)
    neg_inf = jnp.float32(-jnp.inf)
    a_row = -jnp.exp(alog_ref[...])
    n_chunks = proj_ref.shape[0] // CHUNK

    def chunk_body(c, carry):
        rows = pl.ds(pl.multiple_of(c * CHUNK, CHUNK), CHUNK)

        dtp = _softplus(proj_ref[rows, S_DT:S_DT + LANES] + dtb_ref[...])
        cs = _chunk_cumsum(tri3, dtp * a_row)
        cst = cs.T
        dtt = dtp.T
        csb_ref[...] = _lane_replicate(cs, sel_ref[...])
        toendt = jnp.exp(cst[:, CHUNK - 1:CHUNK] - cst) * dtt
        dec_last = jnp.exp(cs[CHUNK - 1:CHUNK, :])

        for g in range(SSD_GROUPS):
            gcols = slice(g * SSD_GW, (g + 1) * SSD_GW)
            bg = proj_ref[rows, S_B + g * SSD_N:S_B + (g + 1) * SSD_N]
            cg = proj_ref[rows, S_C + g * SSD_N:S_C + (g + 1) * SSD_N]
            xg = proj_ref[rows, S_X + g * SSD_GW:S_X + (g + 1) * SSD_GW]
            cbm = lax.dot_general(cg.astype(BF16), bg.astype(BF16), (((1,), (1,)), ((), ())),
                                  preferred_element_type=F32)
            bgt = bg.T
            xb = xg.astype(BF16)
            st_prev = st_ref[g]
            hb = st_prev.astype(BF16)
            decay_row = jnp.zeros((1, SSD_GW), F32)
            lhs, rhs, st_lhs, st_rhs = [], [], [], []
            for j in range(SSD_REP):
                hh = g * SSD_REP + j
                jmask = lane_g // SSD_P == j
                xm = jnp.where(jmask, xb, jnp.zeros_like(xb))
                csb = csb_ref[:, hh * LANES:(hh + 1) * LANES]
                seg = jnp.exp(jnp.where(causal, csb - cst[hh:hh + 1, :], neg_inf))
                lhs += [(cbm * seg * dtt[hh:hh + 1, :]).astype(BF16), (cg * jnp.exp(csb)).astype(BF16)]
                rhs += [xm, jnp.where(jmask, hb, jnp.zeros_like(hb))]
                st_lhs.append((bgt * toendt[hh:hh + 1, :]).astype(BF16))
                st_rhs.append(xm)
                decay_row = jnp.where(jmask, dec_last[:, hh:hh + 1], decay_row)
            yg = dsk_ref[:, gcols] * xg + jnp.dot(jnp.concatenate(lhs, axis=1), jnp.concatenate(rhs, axis=0),
                                                  preferred_element_type=F32)
            st_ref[g] = st_prev * decay_row + jnp.dot(jnp.concatenate(st_lhs, axis=1),
                                                      jnp.concatenate(st_rhs, axis=0), preferred_element_type=F32)
            yz = yg * proj_ref[rows, S_Z + g * SSD_GW:S_Z + (g + 1) * SSD_GW]
            ycat_ref[rows, gcols] = _rmsnorm(yz, nw_ref[:, gcols], 1e-5).astype(BF16)
        return carry

    lax.fori_loop(0, n_chunks, chunk_body, 0)
    o_ref[...] = h_ref[...] + jnp.dot(ycat_ref[...], wout_ref[...], preferred_element_type=F32)


def _ssd_core(proj, h, dtb, alog, dsk, nw_row, wout):
    b, s, d = h.shape
    ts = TS_CORE
    return pl.pallas_call(
        _ssd_core_kernel,
        out_shape=jax.ShapeDtypeStruct((b, s, d), F32),
        grid=(b, s // ts),
        in_specs=[
            pl.BlockSpec((None, ts, S_PROJ), lambda bi, i: (bi, i, 0)),
            pl.BlockSpec((None, ts, d), lambda bi, i: (bi, i, 0)),
            _resident((1, LANES)),
            _resident((1, LANES)),
            _resident((1, SSD_INNER)),
            _resident((1, SSD_INNER)),
            _resident((3 * LANES, SSD_HEADS * LANES)),
            _resident((SSD_INNER, d)),
        ],
        out_specs=pl.BlockSpec((None, ts, d), lambda bi, i: (bi, i, 0)),
        scratch_shapes=[
            pltpu.VMEM((SSD_GROUPS, SSD_N, SSD_GW), F32),
            pltpu.VMEM((CHUNK, SSD_HEADS * LANES), F32),
            pltpu.VMEM((ts, SSD_INNER), BF16),
        ],
        compiler_params=_compiler_params(("parallel", "arbitrary")),
        name="ssd_core",
    )(proj, h, dtb, alog, dsk, nw_row, _replicate_selector(SSD_HEADS), wout)


def _replicate_selector(n_cols):
    sel = jnp.arange(LANES)[:, None] == jnp.arange(n_cols * LANES)[None, :] // LANES
    return jnp.tile(sel.astype(BF16), (3, 1))


def _pad_lanes(row, width=LANES):
    return jnp.pad(row, ((0, 0), (0, width - row.shape[1])))


def _hybrid_layer(h, nw, w_in, i_bias, f_bias, m_norm_w, r_norm_w, w_out, cos_t, sin_t, dec, wq_rep, rrow):
    b, s, d = h.shape
    mq, mk, mv, mi, mf, mo, rq, rk, rv, rg = jnp.split(
        w_in, (256, 512, 1024, 1028, 1032, 1544, 1800, 2056, 2568), axis=1)
    perm = jnp.concatenate([
        (jnp.arange(LANES) // 32) * 64 + jnp.arange(LANES) % 32,
        (jnp.arange(LANES) // 32) * 64 + 32 + jnp.arange(LANES) % 32])
    gate_cols = lambda g: _pad_lanes(jnp.tile(g, (1, 3)))
    w_cat = jnp.concatenate([mq, mk, mv, mo, rq[:, perm], rk[:, perm], rv, rg, gate_cols(mi), gate_cols(mf)],
                            axis=1).astype(BF16)
    gbias = jnp.pad(jnp.concatenate([gate_cols(i_bias[None, :]), gate_cols(f_bias[None, :])], axis=0),
                    ((0, SUBLANES - 2), (0, 0)))
    proj = _norm_matmul(h.reshape(b * s, d), nw[None, :], w_cat).reshape(b, s, H_PROJ)
    nw_row = jnp.concatenate([m_norm_w, r_norm_w])[None, :]
    return _hyb_core(proj, h, cos_t, sin_t, gbias, nw_row, dec, wq_rep, rrow, w_out.astype(BF16))


def _ssd_layer(h, nw, w_in, conv_w, conv_b, dt_bias, a_log, d_skip, norm_w, w_out):
    b, s, d = h.shape
    w_cat = jnp.pad(w_in, ((0, 0), (0, S_PROJ - w_in.shape[1]))).astype(BF16)
    proj = _ssd_proj(h, nw[None, :], w_cat, conv_w, conv_b[None, :])
    dsk = jnp.repeat(d_skip, SSD_P)[None, :]
    return _ssd_core(proj, h, _pad_lanes(dt_bias[None, :]), _pad_lanes(a_log[None, :]), dsk, norm_w[None, :],
                     w_out.astype(BF16))


def _position_tables(seq):
    dk = 64
    inv = ROPE_BASE ** (-jnp.arange(0, dk, 2, dtype=F32) / dk)
    ang = jnp.arange(seq, dtype=F32)[:, None] * inv[None, :]
    cos_t = jnp.tile(jnp.cos(ang), (1, RET_HEADS))
    sin_t = jnp.tile(jnp.sin(ang), (1, RET_HEADS))
    log_gamma = jnp.log(1.0 - 2.0 ** (-5.0 - jnp.arange(RET_HEADS, dtype=F32)))
    idx = jnp.arange(CHUNK, dtype=F32)
    rel = idx[:, None] - idx[None, :]
    dec = jnp.where((rel >= 0)[None], jnp.exp(jnp.maximum(rel, 0.0)[None] * log_gamma[:, None, None]), 0.0)
    w_q = jnp.exp((idx + 1.0)[:, None] * log_gamma)
    w_k = jnp.exp((CHUNK - 1.0 - idx)[:, None] * log_gamma)
    chunk_decay = jnp.broadcast_to(jnp.exp(CHUNK * log_gamma)[:, None], (RET_HEADS, CHUNK))
    tok = _token_of_row(jnp.arange(CHUNK))
    wq_rep = jnp.broadcast_to(w_q[tok].T[:, :, None], (RET_HEADS, CHUNK, HEAD_V))
    rrow = jnp.concatenate([w_k[tok].T, chunk_decay], axis=0)
    return _interleave_tokens(cos_t, 0), _interleave_tokens(sin_t, 0), dec[:, tok][:, :, tok], wq_rep, rrow


def _interleave_tokens(a, axis):
    shape = a.shape
    split = shape[:axis] + (shape[axis] // CHUNK, SUBLANES, CHUNK // SUBLANES) + shape[axis + 1:]
    return jnp.swapaxes(a.reshape(split), axis + 1, axis + 2).reshape(shape)


def _deinterleave_tokens(a, axis):
    shape = a.shape
    split = shape[:axis] + (shape[axis] // CHUNK, CHUNK // SUBLANES, SUBLANES) + shape[axis + 1:]
    return jnp.swapaxes(a.reshape(split), axis + 1, axis + 2).reshape(shape)


@jax.jit
def kernel(x, norm_mix_w, norm_mlp_w, hyb_w_in, mlstm_i_bias, mlstm_f_bias, mlstm_norm_w, ret_norm_w, hyb_w_out,
           ssd_w_in, ssd_conv_w, ssd_conv_b, ssd_dt_bias, ssd_a_log, ssd_d, ssd_norm_w, ssd_w_out, mlp_w1, mlp_w2,
           final_norm_w):
    b, s, d = x.shape
    depth = norm_mix_w.shape[0]
    cos_t, sin_t, dec, wq_rep, rrow = _position_tables(s)
    h = _interleave_tokens(x, 1)
    for layer in range(depth):
        j = layer // 2
        if layer % 2 == 0:
            h = _hybrid_layer(h, norm_mix_w[layer], hyb_w_in[j], mlstm_i_bias[j], mlstm_f_bias[j], mlstm_norm_w[j],
                              ret_norm_w[j], hyb_w_out[j], cos_t, sin_t, dec, wq_rep, rrow)
        else:
            h = _ssd_layer(h, norm_mix_w[layer], ssd_w_in[j], ssd_conv_w[j], ssd_conv_b[j], ssd_dt_bias[j],
                           ssd_a_log[j], ssd_d[j], ssd_norm_w[j], ssd_w_out[j])
        h = _mlp(h.reshape(b * s, d), norm_mlp_w[layer][None, :], mlp_w1[layer].astype(BF16),
                 mlp_w2[layer].astype(BF16), final_norm_w[None, :], layer == depth - 1).reshape(b, s, d)
    return _deinterleave_tokens(h, 1)
```

```python
import functools

import jax
import jax.numpy as jnp
from jax import lax
from jax.experimental import pallas as pl
from jax.experimental.pallas import tpu as pltpu

F32 = jnp.float32
BF16 = jnp.bfloat16

CHUNK = 128
LANES = 128
SUBLANES = 8
VMEM_LIMIT_BYTES = 56 * 1024 * 1024
ROPE_BASE = 10000.0

MLSTM_HEADS = 4
RET_HEADS = 4
HEAD_QK = 64
QK_ALL = 256
HEAD_V = 128
SSD_GROUPS = 8
SSD_REP = 4
SSD_P = 64
SSD_N = 128
SSD_HEADS = SSD_GROUPS * SSD_REP
SSD_GW = SSD_REP * SSD_P
SSD_INNER = SSD_GROUPS * SSD_GW
SSD_CONV = 4

H_MQ, H_MK, H_MV, H_MO = 0, 256, 512, 1024
H_RQ, H_RK, H_RV, H_RG = 1536, 1792, 2048, 2560
H_GI, H_GF = 3072, 3200
H_PROJ = H_GF + LANES
S_Z, S_X, S_B, S_C, S_DT = 0, 2048, 4096, 5120, 6144
S_PROJ = S_DT + LANES
S_CONV_DIM = S_DT - S_X

TM_PROJ = 512
TM_SSD_PROJ = 256
CONV_COLS = 512
HALO_SLABS = SSD_CONV - 1
TM_MLP = 512
TS_CORE = 512
FF_CHUNK = 1024


def _log1p_exp_neg_abs(x):
    return jnp.log1p(jnp.exp(-jnp.abs(x)))


def _softplus(x):
    return jnp.maximum(x, 0.0) + _log1p_exp_neg_abs(x)


def _log_sigmoid(x):
    return jnp.minimum(x, 0.0) - _log1p_exp_neg_abs(x)


def _sigmoid(x):
    return 0.5 * jnp.tanh(0.5 * x) + 0.5


def _silu(x):
    half = 0.5 * x
    return half * jnp.tanh(half) + half


def _rmsnorm(x, w_row, eps):
    return x * lax.rsqrt(jnp.mean(x * x, axis=-1, keepdims=True) + eps) * w_row


def _token_of_row(r):
    return (r % SUBLANES) * (CHUNK // SUBLANES) + r // SUBLANES


def _causal_mask():
    row = lax.broadcasted_iota(jnp.int32, (CHUNK, CHUNK), 0)
    col = lax.broadcasted_iota(jnp.int32, (CHUNK, CHUNK), 1)
    return _token_of_row(row) >= _token_of_row(col)


def _split3(x):
    hi = x.astype(BF16)
    r1 = x - hi.astype(F32)
    mid = r1.astype(BF16)
    lo = (r1 - mid.astype(F32)).astype(BF16)
    return hi, mid, lo


def _chunk_cumsum(tri3, x):
    return jnp.dot(tri3, jnp.concatenate(_split3(x), axis=0), preferred_element_type=F32)


def _lane_replicate(x, sel3):
    return jnp.dot(jnp.concatenate(_split3(x), axis=1), sel3, preferred_element_type=F32)


def _compiler_params(semantics):
    return pltpu.CompilerParams(dimension_semantics=semantics, vmem_limit_bytes=VMEM_LIMIT_BYTES)


def _resident(shape):
    zeros = (0,) * len(shape)
    return pl.BlockSpec(shape, lambda *_: zeros, pipeline_mode=pl.Buffered(1))


def _norm_matmul_kernel(x_ref, nw_ref, w_ref, o_ref):
    xn = _rmsnorm(x_ref[...], nw_ref[...], 1e-6).astype(BF16)
    o_ref[...] = jnp.dot(xn, w_ref[...], preferred_element_type=F32)


def _norm_matmul(x2d, nw_row, w_bf16):
    t, d = x2d.shape
    n = w_bf16.shape[1]
    return pl.pallas_call(
        _norm_matmul_kernel,
        out_shape=jax.ShapeDtypeStruct((t, n), F32),
        grid=(t // TM_PROJ,),
        in_specs=[pl.BlockSpec((TM_PROJ, d), lambda i: (i, 0)), _resident((1, d)), _resident((d, n))],
        out_specs=pl.BlockSpec((TM_PROJ, n), lambda i: (i, 0)),
        compiler_params=_compiler_params(("parallel",)),
        name="norm_matmul",
    )(x2d, nw_row, w_bf16)


def _causal_conv_silu(raw, prev_tail, cw_ref, cb_ref, cc, last_sublane):
    wrapped = []
    for k in range(HALO_SLABS):
        cur = raw[CHUNK - (HALO_SLABS - k) * SUBLANES:CHUNK - (HALO_SLABS - k - 1) * SUBLANES]
        prv = prev_tail[k * SUBLANES:(k + 1) * SUBLANES]
        wrapped.append(pltpu.roll(jnp.where(last_sublane, prv, cur), 1, 0))
    acc = raw * cw_ref[SSD_CONV - 1:SSD_CONV, cc] + cb_ref[:, cc]
    for j in range(1, SSD_CONV):
        shifted = jnp.concatenate(wrapped[HALO_SLABS - j:] + [raw[:CHUNK - j * SUBLANES]], axis=0)
        acc = acc + shifted * cw_ref[SSD_CONV - 1 - j:SSD_CONV - j, cc]
    return _silu(acc)


def _ssd_proj_kernel(x_ref, nw_ref, w_ref, cw_ref, cb_ref, o_ref, halo_ref):
    @pl.when(pl.program_id(1) == 0)
    def _init():
        halo_ref[...] = jnp.zeros_like(halo_ref)

    xn = _rmsnorm(x_ref[...], nw_ref[...], 1e-6).astype(BF16)
    last_sublane = lax.broadcasted_iota(jnp.int32, (SUBLANES, CONV_COLS), 0) == SUBLANES - 1
    n_chunks = x_ref.shape[0] // CHUNK
    tail = slice(CHUNK - HALO_SLABS * SUBLANES, CHUNK)

    for cb in range(S_X // CONV_COLS):
        cols = slice(cb * CONV_COLS, (cb + 1) * CONV_COLS)
        o_ref[:, cols] = _silu(jnp.dot(xn, w_ref[:, cols], preferred_element_type=F32))
    o_ref[:, S_DT:] = jnp.dot(xn, w_ref[:, S_DT:], preferred_element_type=F32)
    for cb in range(S_CONV_DIM // CONV_COLS):
        cc = slice(cb * CONV_COLS, (cb + 1) * CONV_COLS)
        pc = slice(S_X + cb * CONV_COLS, S_X + (cb + 1) * CONV_COLS)
        raw_all = jnp.dot(xn, w_ref[:, pc], preferred_element_type=F32)
        prev_tail = halo_ref[:, cc]
        for c in range(n_chunks):
            raw = raw_all[c * CHUNK:(c + 1) * CHUNK]
            o_ref[c * CHUNK:(c + 1) * CHUNK, pc] = _causal_conv_silu(raw, prev_tail, cw_ref, cb_ref, cc, last_sublane)
            prev_tail = raw[tail]
        halo_ref[:, cc] = prev_tail


def _ssd_proj(h, nw_row, w_bf16, cw, cb_row):
    b, s, d = h.shape
    tm = TM_SSD_PROJ
    return pl.pallas_call(
        _ssd_proj_kernel,
        out_shape=jax.ShapeDtypeStruct((b, s, S_PROJ), F32),
        grid=(b, s // tm),
        in_specs=[pl.BlockSpec((None, tm, d), lambda bi, i: (bi, i, 0)), _resident((1, d)), _resident((d, S_PROJ)),
                  _resident((SSD_CONV, S_CONV_DIM)), _resident((1, S_CONV_DIM))],
        out_specs=pl.BlockSpec((None, tm, S_PROJ), lambda bi, i: (bi, i, 0)),
        scratch_shapes=[pltpu.VMEM((HALO_SLABS * SUBLANES, S_CONV_DIM), F32)],
        compiler_params=_compiler_params(("parallel", "arbitrary")),
        name="ssd_proj",
    )(h, nw_row, w_bf16, cw, cb_row)


def _mlp_kernel(x_ref, nw_ref, w1_ref, w2_ref, fnw_ref, o_ref, *, final):
    x = x_ref[...]
    xn = _rmsnorm(x, nw_ref[...], 1e-6).astype(BF16)
    acc = x
    for f in range(w1_ref.shape[1] // FF_CHUNK):
        cols = slice(f * FF_CHUNK, (f + 1) * FF_CHUNK)
        hid = jnp.maximum(jnp.dot(xn, w1_ref[:, cols], preferred_element_type=F32), 0.0)
        acc = acc + jnp.dot((hid * hid).astype(BF16), w2_ref[cols, :], preferred_element_type=F32)
    if final:
        acc = _rmsnorm(acc, fnw_ref[...], 1e-6)
    o_ref[...] = acc


def _mlp(x2d, nw_row, w1, w2, fnw_row, final):
    t, d = x2d.shape
    dff = w1.shape[1]
    return pl.pallas_call(
        functools.partial(_mlp_kernel, final=final),
        out_shape=jax.ShapeDtypeStruct((t, d), F32),
        grid=(t // TM_MLP,),
        in_specs=[pl.BlockSpec((TM_MLP, d), lambda i: (i, 0)), _resident((1, d)), _resident((d, dff)),
                  _resident((dff, d)), _resident((1, d))],
        out_specs=pl.BlockSpec((TM_MLP, d), lambda i: (i, 0)),
        compiler_params=_compiler_params(("parallel",)),
        name="mlp",
    )(x2d, nw_row, w1, w2, fnw_row)


def _shift_tokens(x, d, fill):
    n_slab = CHUNK // SUBLANES
    sub = lax.broadcasted_iota(jnp.int32, (SUBLANES, x.shape[1]), 0)
    slabs = [x[i * SUBLANES:(i + 1) * SUBLANES] for i in range(n_slab)]
    if d < n_slab:
        wrapped = [jnp.where(sub == 0, fill, pltpu.roll(slabs[n_slab - d + i], 1, 0)) for i in range(d)]
        return jnp.concatenate(wrapped + slabs[:n_slab - d], axis=0)
    k = d // n_slab
    return jnp.concatenate([jnp.where(sub < k, fill, pltpu.roll(sl, k, 0)) for sl in slabs], axis=0)


def _token_cummax(x):
    d = 1
    while d < CHUNK:
        x = jnp.maximum(x, _shift_tokens(x, d, -jnp.inf))
        d *= 2
    return x


def _split2(x):
    hi = x.astype(BF16)
    return hi, (x - hi.astype(F32)).astype(BF16)


def _row_mean_replicated(x, j2):
    return jnp.dot(jnp.concatenate(_split2(x), axis=1), j2, preferred_element_type=F32)


def _embed_rows(block, starts):
    piece = block.shape[0] // len(starts)
    parts, pos = [], 0
    for i, start in enumerate(starts):
        if start > pos:
            parts.append(jnp.zeros((start - pos, block.shape[1]), block.dtype))
        parts.append(block[i * piece:(i + 1) * piece])
        pos = start + piece
    if pos < QK_ALL:
        parts.append(jnp.zeros((QK_ALL - pos, block.shape[1]), block.dtype))
    return jnp.concatenate(parts, axis=0)


N_REP = 2 * MLSTM_HEADS


def _hyb_core_kernel(proj_ref, h_ref, cos_ref, sin_ref, gbias_ref, nw_ref, dec_ref, wq_ref, rrow_ref, sel_ref,
                     wout_ref, o_ref, cst_ref, mst_ref, rst_ref, rep_ref, x_ref, dn_ref, ycat_ref):
    @pl.when(pl.program_id(1) == 0)
    def _init():
        cst_ref[...] = jnp.zeros_like(cst_ref)
        mst_ref[...] = jnp.zeros_like(mst_ref)
        rst_ref[...] = jnp.zeros_like(rst_ref)

    causal = _causal_mask()
    tri3 = jnp.tile(causal.astype(BF16), (1, 3))
    lane_qk = lax.broadcasted_iota(jnp.int32, (1, QK_ALL), 1)
    lane = lax.broadcasted_iota(jnp.int32, (CHUNK, LANES), 1)
    ones_v = jnp.ones((CHUNK, HEAD_V), BF16)
    j2 = jnp.full((2 * HEAD_V, HEAD_V), 1.0 / HEAD_V, BF16)
    neg_inf = jnp.float32(-jnp.inf)
    n_chunks = proj_ref.shape[0] // CHUNK
    n_heads = MLSTM_HEADS + RET_HEADS
    half = HEAD_QK // 2

    def finish_chunk(rows):
        x = x_ref[...]
        d = x - _row_mean_replicated(x, j2)
        var = _row_mean_replicated(d * d, j2)
        dn = dn_ref[...]
        eps = jnp.concatenate([1e-5 * dn * dn, jnp.full((RET_HEADS * CHUNK, HEAD_V), 1e-5, F32)], axis=0)
        y = d * lax.rsqrt(var + eps)
        for h in range(n_heads):
            cols = slice(h * HEAD_V, (h + 1) * HEAD_V)
            if h < MLSTM_HEADS:
                gate = _sigmoid(proj_ref[rows, H_MO + h * HEAD_V:H_MO + (h + 1) * HEAD_V])
            else:
                hr = h - MLSTM_HEADS
                gate = _silu(proj_ref[rows, H_RG + hr * HEAD_V:H_RG + (hr + 1) * HEAD_V])
            ycat_ref[rows, cols] = (gate * (y[h * CHUNK:(h + 1) * CHUNK] * nw_ref[:, cols])).astype(BF16)

    def chunk_body(c, carry):
        rows = pl.ds(pl.multiple_of(c * CHUNK, CHUNK), CHUNK)

        gi = proj_ref[rows, H_GI:H_GI + LANES] + gbias_ref[0:1, :]
        gf = proj_ref[rows, H_GF:H_GF + LANES] + gbias_ref[1:2, :]
        bc = _chunk_cumsum(tri3, _log_sigmoid(gf))
        r = gi - bc
        m_prev = mst_ref[0:1, :]
        mu = jnp.maximum(m_prev, _token_cummax(r))
        g_row = bc[CHUNK - 1:CHUNK, :]
        m_new = jnp.maximum(g_row + m_prev, jnp.max(g_row + r, axis=0, keepdims=True))
        s_old = jnp.exp(g_row + m_prev - m_new)
        wt = jnp.exp(g_row + r - m_new).T
        rt = (r - m_prev).T
        mst_ref[0:1, :] = m_new
        packed = jnp.where(lane < MLSTM_HEADS, mu - m_prev, bc + mu)
        rep_ref[...] = _lane_replicate(packed, sel_ref[...])

        mq = (proj_ref[rows, H_MQ:H_MQ + QK_ALL] * 0.125).astype(BF16)
        mkt_f = proj_ref[rows, H_MK:H_MK + QK_ALL].T
        qhs = [jnp.where(lane_qk // HEAD_QK == h, mq, jnp.zeros_like(mq)) for h in range(MLSTM_HEADS)]
        scores = jnp.dot(jnp.concatenate(qhs, axis=0), mkt_f.astype(BF16), preferred_element_type=F32)
        for h in range(MLSTM_HEADS):
            hrows = slice(h * CHUNK, (h + 1) * CHUNK)
            alpha_b = rep_ref[:, h * LANES:(h + 1) * LANES]
            beta_b = rep_ref[:, (MLSTM_HEADS + h) * LANES:(MLSTM_HEADS + h + 1) * LANES]
            c_prev = cst_ref[h]
            vext = jnp.concatenate(
                [proj_ref[rows, H_MV + h * HEAD_V:H_MV + (h + 1) * HEAD_V].astype(BF16), ones_v], axis=1)

            dmat = jnp.exp(jnp.where(causal, rt[h:h + 1, :] - alpha_b, neg_inf))
            s = scores[hrows] * dmat
            kw = (mkt_f[h * HEAD_QK:(h + 1) * HEAD_QK] * wt[h:h + 1, :]).astype(BF16)
            both = jnp.dot(jnp.concatenate([s.astype(BF16), kw], axis=0), vext, preferred_element_type=F32)
            inter = jnp.dot(qhs[h], _embed_rows(c_prev.astype(BF16), [h * HEAD_QK]),
                            preferred_element_type=F32)
            si_b = jnp.exp(-alpha_b)
            num_ext = both[:CHUNK] + jnp.concatenate([si_b, si_b], axis=1) * inter
            x_ref[hrows, :] = num_ext[:, :HEAD_V]
            dn_ref[hrows, :] = jnp.maximum(jnp.abs(num_ext[:, HEAD_V:]), jnp.exp(-beta_b))
            cst_ref[h] = s_old[:, h:h + 1] * c_prev + both[CHUNK:]

        cs, sn = cos_ref[rows, :], sin_ref[rows, :]
        q1, q2 = proj_ref[rows, H_RQ:H_RQ + LANES], proj_ref[rows, H_RQ + LANES:H_RQ + 2 * LANES]
        k1, k2 = proj_ref[rows, H_RK:H_RK + LANES], proj_ref[rows, H_RK + LANES:H_RK + 2 * LANES]
        rq = (jnp.concatenate([q1 * cs - q2 * sn, q1 * sn + q2 * cs], axis=1) * 0.125).astype(BF16)
        rkt_f = jnp.concatenate([k1 * cs - k2 * sn, k1 * sn + k2 * cs], axis=1).T
        qhs = [jnp.where((lane_qk % LANES) // half == h, rq, jnp.zeros_like(rq)) for h in range(RET_HEADS)]
        scores = jnp.dot(jnp.concatenate(qhs, axis=0), rkt_f.astype(BF16), preferred_element_type=F32)
        for h in range(RET_HEADS):
            hrows = slice((MLSTM_HEADS + h) * CHUNK, (MLSTM_HEADS + h + 1) * CHUNK)
            v = proj_ref[rows, H_RV + h * HEAD_V:H_RV + (h + 1) * HEAD_V].astype(BF16)
            r_prev = rst_ref[h]
            s = scores[h * CHUNK:(h + 1) * CHUNK] * dec_ref[h]
            k_own = jnp.concatenate([rkt_f[h * half:(h + 1) * half], rkt_f[LANES + h * half:LANES + (h + 1) * half]],
                                    axis=0)
            kw = (k_own * rrow_ref[h:h + 1, :]).astype(BF16)
            both = jnp.dot(jnp.concatenate([s.astype(BF16), kw], axis=0), v, preferred_element_type=F32)
            inter = jnp.dot(qhs[h], _embed_rows(r_prev.astype(BF16), [h * half, LANES + h * half]),
                            preferred_element_type=F32)
            x_ref[hrows, :] = both[:CHUNK] + wq_ref[h] * inter
            rst_ref[h] = rrow_ref[RET_HEADS + h:RET_HEADS + h + 1, 0:1] * r_prev + both[CHUNK:]

        finish_chunk(rows)
        return carry

    lax.fori_loop(0, n_chunks, chunk_body, 0)
    o_ref[...] = h_ref[...] + jnp.dot(ycat_ref[...], wout_ref[...], preferred_element_type=F32)


def _hyb_core(proj, h, cos_t, sin_t, gbias, nw_row, dec, wq_rep, rrow, wout):
    b, s, d = h.shape
    ts = TS_CORE
    inner = wout.shape[0]
    n_heads = MLSTM_HEADS + RET_HEADS
    return pl.pallas_call(
        _hyb_core_kernel,
        out_shape=jax.ShapeDtypeStruct((b, s, d), F32),
        grid=(b, s // ts),
        in_specs=[
            pl.BlockSpec((None, ts, H_PROJ), lambda bi, i: (bi, i, 0)),
            pl.BlockSpec((None, ts, d), lambda bi, i: (bi, i, 0)),
            pl.BlockSpec((ts, LANES), lambda bi, i: (i, 0)),
            pl.BlockSpec((ts, LANES), lambda bi, i: (i, 0)),
            _resident((SUBLANES, LANES)),
            _resident((1, inner)),
            _resident((RET_HEADS, CHUNK, CHUNK)),
            _resident((RET_HEADS, CHUNK, HEAD_V)),
            _resident((SUBLANES, LANES)),
            _resident((3 * LANES, N_REP * LANES)),
            _resident((inner, d)),
        ],
        out_specs=pl.BlockSpec((None, ts, d), lambda bi, i: (bi, i, 0)),
        scratch_shapes=[
            pltpu.VMEM((MLSTM_HEADS, HEAD_QK, 2 * HEAD_V), F32),
            pltpu.VMEM((SUBLANES, LANES), F32),
            pltpu.VMEM((RET_HEADS, HEAD_QK, HEAD_V), F32),
            pltpu.VMEM((CHUNK, N_REP * LANES), F32),
            pltpu.VMEM((n_heads * CHUNK, HEAD_V), F32),
            pltpu.VMEM((MLSTM_HEADS * CHUNK, HEAD_V), F32),
            pltpu.VMEM((ts, inner), BF16),
        ],
        compiler_params=_compiler_params(("parallel", "arbitrary")),
        name="hybrid_core",
    )(proj, h, cos_t, sin_t, gbias, nw_row, dec, wq_rep, rrow, _replicate_selector(N_REP), wout)


def _ssd_core_kernel(proj_ref, h_ref, dtb_ref, alog_ref, dsk_ref, nw_ref, sel_ref, wout_ref, o_ref,
                     st_ref, csb_ref, ycat_ref):
    @pl.when(pl.program_id(1) == 0)
    def _init():
        st_ref[...] = jnp.zeros_like(st_ref)

    causal = _causal_mask()
    tri3 = jnp.tile(causal.astype(BF16), (1, 3))
    lane_g = lax.broadcasted_iota(jnp.int32, (1, SSD_GW), 1)
    neg_inf = jnp.float32(-jnp.inf)
    a_row = -jnp.exp(alog_ref[...])
    n_chunks = proj_ref.shape[0] // CHUNK

    def chunk_body(c, carry):
        rows = pl.ds(pl.multiple_of(c * CHUNK, CHUNK), CHUNK)

        dtp = _softplus(proj_ref[rows, S_DT:S_DT + LANES] + dtb_ref[...])
        cs = _chunk_cumsum(tri3, dtp * a_row)
        cst = cs.T
        dtt = dtp.T
        csb_ref[...] = _lane_replicate(cs, sel_ref[...])
        toendt = jnp.exp(cst[:, CHUNK - 1:CHUNK] - cst) * dtt
        dec_last = jnp.exp(cs[CHUNK - 1:CHUNK, :])

        for g in range(SSD_GROUPS):
            gcols = slice(g * SSD_GW, (g + 1) * SSD_GW)
            bg = proj_ref[rows, S_B + g * SSD_N:S_B + (g + 1) * SSD_N]
            cg = proj_ref[rows, S_C + g * SSD_N:S_C + (g + 1) * SSD_N]
            xg = proj_ref[rows, S_X + g * SSD_GW:S_X + (g + 1) * SSD_GW]
            cbm = lax.dot_general(cg.astype(BF16), bg.astype(BF16), (((1,), (1,)), ((), ())),
                                  preferred_element_type=F32)
            bgt = bg.T
            xb = xg.astype(BF16)
            st_prev = st_ref[g]
            hb = st_prev.astype(BF16)
            decay_row = jnp.zeros((1, SSD_GW), F32)
            lhs, rhs, st_lhs, st_rhs = [], [], [], []
            for j in range(SSD_REP):
                hh = g * SSD_REP + j
                jmask = lane_g // SSD_P == j
                xm = jnp.where(jmask, xb, jnp.zeros_like(xb))
                csb = csb_ref[:, hh * LANES:(hh + 1) * LANES]
                seg = jnp.exp(jnp.where(causal, csb - cst[hh:hh + 1, :], neg_inf))
                lhs += [(cbm * seg * dtt[hh:hh + 1, :]).astype(BF16), (cg * jnp.exp(csb)).astype(BF16)]
                rhs += [xm, jnp.where(jmask, hb, jnp.zeros_like(hb))]
                st_lhs.append((bgt * toendt[hh:hh + 1, :]).astype(BF16))
                st_rhs.append(xm)
                decay_row = jnp.where(jmask, dec_last[:, hh:hh + 1], decay_row)
            yg = dsk_ref[:, gcols] * xg + jnp.dot(jnp.concatenate(lhs, axis=1), jnp.concatenate(rhs, axis=0),
                                                  preferred_element_type=F32)
            st_ref[g] = st_prev * decay_row + jnp.dot(jnp.concatenate(st_lhs, axis=1),
                                                      jnp.concatenate(st_rhs, axis=0), preferred_element_type=F32)
            yz = yg * proj_ref[rows, S_Z + g * SSD_GW:S_Z + (g + 1) * SSD_GW]
            ycat_ref[rows, gcols] = _rmsnorm(yz, nw_ref[:, gcols], 1e-5).astype(BF16)
        return carry

    lax.fori_loop(0, n_chunks, chunk_body, 0)
    o_ref[...] = h_ref[...] + jnp.dot(ycat_ref[...], wout_ref[...], preferred_element_type=F32)


def _ssd_core(proj, h, dtb, alog, dsk, nw_row, wout):
    b, s, d = h.shape
    ts = TS_CORE
    return pl.pallas_call(
        _ssd_core_kernel,
        out_shape=jax.ShapeDtypeStruct((b, s, d), F32),
        grid=(b, s // ts),
        in_specs=[
            pl.BlockSpec((None, ts, S_PROJ), lambda bi, i: (bi, i, 0)),
            pl.BlockSpec((None, ts, d), lambda bi, i: (bi, i, 0)),
            _resident((1, LANES)),
            _resident((1, LANES)),
            _resident((1, SSD_INNER)),
            _resident((1, SSD_INNER)),
            _resident((3 * LANES, SSD_HEADS * LANES)),
            _resident((SSD_INNER, d)),
        ],
        out_specs=pl.BlockSpec((None, ts, d), lambda bi, i: (bi, i, 0)),
        scratch_shapes=[
            pltpu.VMEM((SSD_GROUPS, SSD_N, SSD_GW), F32),
            pltpu.VMEM((CHUNK, SSD_HEADS * LANES), F32),
            pltpu.VMEM((ts, SSD_INNER), BF16),
        ],
        compiler_params=_compiler_params(("parallel", "arbitrary")),
        name="ssd_core",
    )(proj, h, dtb, alog, dsk, nw_row, _replicate_selector(SSD_HEADS), wout)


def _replicate_selector(n_cols):
    sel = jnp.arange(LANES)[:, None] == jnp.arange(n_cols * LANES)[None, :] // LANES
    return jnp.tile(sel.astype(BF16), (3, 1))


def _pad_lanes(row, width=LANES):
    return jnp.pad(row, ((0, 0), (0, width - row.shape[1])))


def _hybrid_layer(h, nw, w_in, i_bias, f_bias, m_norm_w, r_norm_w, w_out, cos_t, sin_t, dec, wq_rep, rrow):
    b, s, d = h.shape
    mq, mk, mv, mi, mf, mo, rq, rk, rv, rg = jnp.split(
        w_in, (256, 512, 1024, 1028, 1032, 1544, 1800, 2056, 2568), axis=1)
    perm = jnp.concatenate([
        (jnp.arange(LANES) // 32) * 64 + jnp.arange(LANES) % 32,
        (jnp.arange(LANES) // 32) * 64 + 32 + jnp.arange(LANES) % 32])
    gate_cols = lambda g: _pad_lanes(jnp.tile(g, (1, 2)))
    w_cat = jnp.concatenate([mq, mk, mv, mo, rq[:, perm], rk[:, perm], rv, rg, gate_cols(mi), gate_cols(mf)],
                            axis=1).astype(BF16)
    gbias = jnp.pad(jnp.concatenate([gate_cols(i_bias[None, :]), gate_cols(f_bias[None, :])], axis=0),
                    ((0, SUBLANES - 2), (0, 0)))
    proj = _norm_matmul(h.reshape(b * s, d), nw[None, :], w_cat).reshape(b, s, H_PROJ)
    nw_row = jnp.concatenate([m_norm_w, r_norm_w])[None, :]
    return _hyb_core(proj, h, cos_t, sin_t, gbias, nw_row, dec, wq_rep, rrow, w_out.astype(BF16))


def _ssd_layer(h, nw, w_in, conv_w, conv_b, dt_bias, a_log, d_skip, norm_w, w_out):
    b, s, d = h.shape
    w_cat = jnp.pad(w_in, ((0, 0), (0, S_PROJ - w_in.shape[1]))).astype(BF16)
    proj = _ssd_proj(h, nw[None, :], w_cat, conv_w, conv_b[None, :])
    dsk = jnp.repeat(d_skip, SSD_P)[None, :]
    return _ssd_core(proj, h, _pad_lanes(dt_bias[None, :]), _pad_lanes(a_log[None, :]), dsk, norm_w[None, :],
                     w_out.astype(BF16))


def _position_tables(seq):
    dk = 64
    inv = ROPE_BASE ** (-jnp.arange(0, dk, 2, dtype=F32) / dk)
    ang = jnp.arange(seq, dtype=F32)[:, None] * inv[None, :]
    cos_t = jnp.tile(jnp.cos(ang), (1, RET_HEADS))
    sin_t = jnp.tile(jnp.sin(ang), (1, RET_HEADS))
    log_gamma = jnp.log(1.0 - 2.0 ** (-5.0 - jnp.arange(RET_HEADS, dtype=F32)))
    idx = jnp.arange(CHUNK, dtype=F32)
    rel = idx[:, None] - idx[None, :]
    dec = jnp.where((rel >= 0)[None], jnp.exp(jnp.maximum(rel, 0.0)[None] * log_gamma[:, None, None]), 0.0)
    w_q = jnp.exp((idx + 1.0)[:, None] * log_gamma)
    w_k = jnp.exp((CHUNK - 1.0 - idx)[:, None] * log_gamma)
    chunk_decay = jnp.broadcast_to(jnp.exp(CHUNK * log_gamma)[:, None], (RET_HEADS, CHUNK))
    tok = _token_of_row(jnp.arange(CHUNK))
    wq_rep = jnp.broadcast_to(w_q[tok].T[:, :, None], (RET_HEADS, CHUNK, HEAD_V))
    rrow = jnp.concatenate([w_k[tok].T, chunk_decay], axis=0)
    return _interleave_tokens(cos_t, 0), _interleave_tokens(sin_t, 0), dec[:, tok][:, :, tok], wq_rep, rrow


def _interleave_tokens(a, axis):
    shape = a.shape
    split = shape[:axis] + (shape[axis] // CHUNK, SUBLANES, CHUNK // SUBLANES) + shape[axis + 1:]
    return jnp.swapaxes(a.reshape(split), axis + 1, axis + 2).reshape(shape)


def _deinterleave_tokens(a, axis):
    shape = a.shape
    split = shape[:axis] + (shape[axis] // CHUNK, CHUNK // SUBLANES, SUBLANES) + shape[axis + 1:]
    return jnp.swapaxes(a.reshape(split), axis + 1, axis + 2).reshape(shape)


@jax.jit
def kernel(x, norm_mix_w, norm_mlp_w, hyb_w_in, mlstm_i_bias, mlstm_f_bias, mlstm_norm_w, ret_norm_w, hyb_w_out,
           ssd_w_in, ssd_conv_w, ssd_conv_b, ssd_dt_bias, ssd_a_log, ssd_d, ssd_norm_w, ssd_w_out, mlp_w1, mlp_w2,
           final_norm_w):
    b, s, d = x.shape
    depth = norm_mix_w.shape[0]
    cos_t, sin_t, dec, wq_rep, rrow = _position_tables(s)
    h = _interleave_tokens(x, 1)
    for layer in range(depth):
        j = layer // 2
        if layer % 2 == 0:
            h = _hybrid_layer(h, norm_mix_w[layer], hyb_w_in[j], mlstm_i_bias[j], mlstm_f_bias[j], mlstm_norm_w[j],
                              ret_norm_w[j], hyb_w_out[j], cos_t, sin_t, dec, wq_rep, rrow)
        else:
            h = _ssd_layer(h, norm_mix_w[layer], ssd_w_in[j], ssd_conv_w[j], ssd_conv_b[j], ssd_dt_bias[j],
                           ssd_a_log[j], ssd_d[j], ssd_norm_w[j], ssd_w_out[j])
        h = _mlp(h.reshape(b * s, d), norm_mlp_w[layer][None, :], mlp_w1[layer].astype(BF16),
                 mlp_w2[layer].astype(BF16), final_norm_w[None, :], layer == depth - 1).reshape(b, s, d)
    return _deinterleave_tokens(h, 1)
```

```python
import functools

import jax
import jax.numpy as jnp
from jax import lax
from jax.experimental import pallas as pl
from jax.experimental.pallas import tpu as pltpu

F32 = jnp.float32
BF16 = jnp.bfloat16

CHUNK = 128
LANES = 128
SUBLANES = 8
VMEM_LIMIT_BYTES = 56 * 1024 * 1024
ROPE_BASE = 10000.0

MLSTM_HEADS = 4
RET_HEADS = 4
HEAD_QK = 64
QK_ALL = 256
HEAD_V = 128
SSD_GROUPS = 8
SSD_REP = 4
SSD_P = 64
SSD_N = 128
SSD_HEADS = SSD_GROUPS * SSD_REP
SSD_GW = SSD_REP * SSD_P
SSD_INNER = SSD_GROUPS * SSD_GW
SSD_CONV = 4

H_MQ, H_MK, H_MV, H_MO = 0, 256, 512, 1024
H_RQ, H_RK, H_RV, H_RG = 1536, 1792, 2048, 2560
H_GI, H_GF = 3072, 3200
H_PROJ = H_GF + LANES
S_Z, S_X, S_B, S_C, S_DT = 0, 2048, 4096, 5120, 6144
S_PROJ = S_DT + LANES
S_CONV_DIM = S_DT - S_X

TM_PROJ = 512
TM_SSD_PROJ = 256
CONV_COLS = 512
HALO_SLABS = SSD_CONV - 1
TM_MLP = 512
TS_CORE = 512
FF_CHUNK = 1024


def _log1p_exp_neg_abs(x):
    return jnp.log1p(jnp.exp(-jnp.abs(x)))


def _softplus(x):
    return jnp.maximum(x, 0.0) + _log1p_exp_neg_abs(x)


def _log_sigmoid(x):
    return jnp.minimum(x, 0.0) - _log1p_exp_neg_abs(x)


def _sigmoid(x):
    return 0.5 * jnp.tanh(0.5 * x) + 0.5


def _silu(x):
    half = 0.5 * x
    return half * jnp.tanh(half) + half


def _rmsnorm(x, w_row, eps):
    return x * lax.rsqrt(jnp.mean(x * x, axis=-1, keepdims=True) + eps) * w_row


def _token_of_row(r):
    return (r % SUBLANES) * (CHUNK // SUBLANES) + r // SUBLANES


def _causal_mask():
    row = lax.broadcasted_iota(jnp.int32, (CHUNK, CHUNK), 0)
    col = lax.broadcasted_iota(jnp.int32, (CHUNK, CHUNK), 1)
    return _token_of_row(row) >= _token_of_row(col)


def _split3(x):
    hi = x.astype(BF16)
    r1 = x - hi.astype(F32)
    mid = r1.astype(BF16)
    lo = (r1 - mid.astype(F32)).astype(BF16)
    return hi, mid, lo


def _chunk_cumsum(tri3, x):
    return jnp.dot(tri3, jnp.concatenate(_split3(x), axis=0), preferred_element_type=F32)


def _lane_replicate(x, sel, n_cols):
    hi, mid, lo = _split3(x)
    lane = lax.broadcasted_iota(jnp.int32, x.shape, 1)
    packed = jnp.where(lane < n_cols, hi, jnp.where(lane < 2 * n_cols, mid, lo))
    return jnp.dot(packed, sel, preferred_element_type=F32)


def _compiler_params(semantics):
    return pltpu.CompilerParams(dimension_semantics=semantics, vmem_limit_bytes=VMEM_LIMIT_BYTES)


def _resident(shape):
    zeros = (0,) * len(shape)
    return pl.BlockSpec(shape, lambda *_: zeros, pipeline_mode=pl.Buffered(1))


def _norm_matmul_kernel(x_ref, nw_ref, w_ref, o_ref):
    xn = _rmsnorm(x_ref[...], nw_ref[...], 1e-6).astype(BF16)
    o_ref[...] = jnp.dot(xn, w_ref[...], preferred_element_type=F32)


def _norm_matmul(x2d, nw_row, w_bf16):
    t, d = x2d.shape
    n = w_bf16.shape[1]
    return pl.pallas_call(
        _norm_matmul_kernel,
        out_shape=jax.ShapeDtypeStruct((t, n), F32),
        grid=(t // TM_PROJ,),
        in_specs=[pl.BlockSpec((TM_PROJ, d), lambda i: (i, 0)), _resident((1, d)), _resident((d, n))],
        out_specs=pl.BlockSpec((TM_PROJ, n), lambda i: (i, 0)),
        compiler_params=_compiler_params(("parallel",)),
        name="norm_matmul",
    )(x2d, nw_row, w_bf16)


def _causal_conv_silu(raw, prev_tail, cw_ref, cb_ref, cc, last_sublane):
    wrapped = []
    for k in range(HALO_SLABS):
        cur = raw[CHUNK - (HALO_SLABS - k) * SUBLANES:CHUNK - (HALO_SLABS - k - 1) * SUBLANES]
        prv = prev_tail[k * SUBLANES:(k + 1) * SUBLANES]
        wrapped.append(pltpu.roll(jnp.where(last_sublane, prv, cur), 1, 0))
    acc = raw * cw_ref[SSD_CONV - 1:SSD_CONV, cc] + cb_ref[:, cc]
    for j in range(1, SSD_CONV):
        shifted = jnp.concatenate(wrapped[HALO_SLABS - j:] + [raw[:CHUNK - j * SUBLANES]], axis=0)
        acc = acc + shifted * cw_ref[SSD_CONV - 1 - j:SSD_CONV - j, cc]
    return _silu(acc)


def _ssd_proj_kernel(x_ref, nw_ref, w_ref, cw_ref, cb_ref, o_ref, halo_ref):
    @pl.when(pl.program_id(1) == 0)
    def _init():
        halo_ref[...] = jnp.zeros_like(halo_ref)

    xn = _rmsnorm(x_ref[...], nw_ref[...], 1e-6).astype(BF16)
    last_sublane = lax.broadcasted_iota(jnp.int32, (SUBLANES, CONV_COLS), 0) == SUBLANES - 1
    n_chunks = x_ref.shape[0] // CHUNK
    tail = slice(CHUNK - HALO_SLABS * SUBLANES, CHUNK)

    for cb in range(S_X // CONV_COLS):
        cols = slice(cb * CONV_COLS, (cb + 1) * CONV_COLS)
        o_ref[:, cols] = _silu(jnp.dot(xn, w_ref[:, cols], preferred_element_type=F32))
    o_ref[:, S_DT:] = jnp.dot(xn, w_ref[:, S_DT:], preferred_element_type=F32)
    for cb in range(S_CONV_DIM // CONV_COLS):
        cc = slice(cb * CONV_COLS, (cb + 1) * CONV_COLS)
        pc = slice(S_X + cb * CONV_COLS, S_X + (cb + 1) * CONV_COLS)
        raw_all = jnp.dot(xn, w_ref[:, pc], preferred_element_type=F32)
        prev_tail = halo_ref[:, cc]
        for c in range(n_chunks):
            raw = raw_all[c * CHUNK:(c + 1) * CHUNK]
            o_ref[c * CHUNK:(c + 1) * CHUNK, pc] = _causal_conv_silu(raw, prev_tail, cw_ref, cb_ref, cc, last_sublane)
            prev_tail = raw[tail]
        halo_ref[:, cc] = prev_tail


def _ssd_proj(h, nw_row, w_bf16, cw, cb_row):
    b, s, d = h.shape
    tm = TM_SSD_PROJ
    return pl.pallas_call(
        _ssd_proj_kernel,
        out_shape=jax.ShapeDtypeStruct((b, s, S_PROJ), F32),
        grid=(b, s // tm),
        in_specs=[pl.BlockSpec((None, tm, d), lambda bi, i: (bi, i, 0)), _resident((1, d)), _resident((d, S_PROJ)),
                  _resident((SSD_CONV, S_CONV_DIM)), _resident((1, S_CONV_DIM))],
        out_specs=pl.BlockSpec((None, tm, S_PROJ), lambda bi, i: (bi, i, 0)),
        scratch_shapes=[pltpu.VMEM((HALO_SLABS * SUBLANES, S_CONV_DIM), F32)],
        compiler_params=_compiler_params(("parallel", "arbitrary")),
        name="ssd_proj",
    )(h, nw_row, w_bf16, cw, cb_row)


def _mlp_kernel(x_ref, nw_ref, w1_ref, w2_ref, fnw_ref, o_ref, *, final):
    x = x_ref[...]
    xn = _rmsnorm(x, nw_ref[...], 1e-6).astype(BF16)
    acc = x
    for f in range(w1_ref.shape[1] // FF_CHUNK):
        cols = slice(f * FF_CHUNK, (f + 1) * FF_CHUNK)
        hid = jnp.maximum(jnp.dot(xn, w1_ref[:, cols], preferred_element_type=F32), 0.0)
        acc = acc + jnp.dot((hid * hid).astype(BF16), w2_ref[cols, :], preferred_element_type=F32)
    if final:
        acc = _rmsnorm(acc, fnw_ref[...], 1e-6)
    o_ref[...] = acc


def _mlp(x2d, nw_row, w1, w2, fnw_row, final):
    t, d = x2d.shape
    dff = w1.shape[1]
    return pl.pallas_call(
        functools.partial(_mlp_kernel, final=final),
        out_shape=jax.ShapeDtypeStruct((t, d), F32),
        grid=(t // TM_MLP,),
        in_specs=[pl.BlockSpec((TM_MLP, d), lambda i: (i, 0)), _resident((1, d)), _resident((d, dff)),
                  _resident((dff, d)), _resident((1, d))],
        out_specs=pl.BlockSpec((TM_MLP, d), lambda i: (i, 0)),
        compiler_params=_compiler_params(("parallel",)),
        name="mlp",
    )(x2d, nw_row, w1, w2, fnw_row)


def _shift_tokens(x, d, fill):
    n_slab = CHUNK // SUBLANES
    sub = lax.broadcasted_iota(jnp.int32, (SUBLANES, x.shape[1]), 0)
    slabs = [x[i * SUBLANES:(i + 1) * SUBLANES] for i in range(n_slab)]
    if d < n_slab:
        wrapped = [jnp.where(sub == 0, fill, pltpu.roll(slabs[n_slab - d + i], 1, 0)) for i in range(d)]
        return jnp.concatenate(wrapped + slabs[:n_slab - d], axis=0)
    k = d // n_slab
    return jnp.concatenate([jnp.where(sub < k, fill, pltpu.roll(sl, k, 0)) for sl in slabs], axis=0)


def _token_cummax(x):
    d = 1
    while d < CHUNK:
        x = jnp.maximum(x, _shift_tokens(x, d, -jnp.inf))
        d *= 2
    return x


def _split2(x):
    hi = x.astype(BF16)
    return hi, (x - hi.astype(F32)).astype(BF16)


def _row_mean_replicated(x, j2):
    return jnp.dot(jnp.concatenate(_split2(x), axis=1), j2, preferred_element_type=F32)


def _embed_rows(block, starts):
    piece = block.shape[0] // len(starts)
    parts, pos = [], 0
    for i, start in enumerate(starts):
        if start > pos:
            parts.append(jnp.zeros((start - pos, block.shape[1]), block.dtype))
        parts.append(block[i * piece:(i + 1) * piece])
        pos = start + piece
    if pos < QK_ALL:
        parts.append(jnp.zeros((QK_ALL - pos, block.shape[1]), block.dtype))
    return jnp.concatenate(parts, axis=0)


N_REP = 2 * MLSTM_HEADS


def _hyb_core_kernel(proj_ref, h_ref, cos_ref, sin_ref, gbias_ref, nw_ref, dec_ref, wq_ref, rrow_ref, sel_ref,
                     wout_ref, o_ref, cst_ref, mst_ref, rst_ref, rep_ref, x_ref, dn_ref, ycat_ref):
    @pl.when(pl.program_id(1) == 0)
    def _init():
        cst_ref[...] = jnp.zeros_like(cst_ref)
        mst_ref[...] = jnp.zeros_like(mst_ref)
        rst_ref[...] = jnp.zeros_like(rst_ref)

    causal = _causal_mask()
    tri3 = jnp.tile(causal.astype(BF16), (1, 3))
    lane_qk = lax.broadcasted_iota(jnp.int32, (1, QK_ALL), 1)
    lane = lax.broadcasted_iota(jnp.int32, (CHUNK, LANES), 1)
    ones_v = jnp.ones((CHUNK, HEAD_V), BF16)
    j2 = jnp.full((2 * HEAD_V, HEAD_V), 1.0 / HEAD_V, BF16)
    neg_inf = jnp.float32(-jnp.inf)
    n_chunks = proj_ref.shape[0] // CHUNK
    n_heads = MLSTM_HEADS + RET_HEADS
    half = HEAD_QK // 2

    def finish_chunk(rows):
        x = x_ref[...]
        d = x - _row_mean_replicated(x, j2)
        var = _row_mean_replicated(d * d, j2)
        dn = dn_ref[...]
        eps = jnp.concatenate([1e-5 * dn * dn, jnp.full((RET_HEADS * CHUNK, HEAD_V), 1e-5, F32)], axis=0)
        y = d * lax.rsqrt(var + eps)
        for h in range(n_heads):
            cols = slice(h * HEAD_V, (h + 1) * HEAD_V)
            if h < MLSTM_HEADS:
                gate = _sigmoid(proj_ref[rows, H_MO + h * HEAD_V:H_MO + (h + 1) * HEAD_V])
            else:
                hr = h - MLSTM_HEADS
                gate = _silu(proj_ref[rows, H_RG + hr * HEAD_V:H_RG + (hr + 1) * HEAD_V])
            ycat_ref[rows, cols] = (gate * (y[h * CHUNK:(h + 1) * CHUNK] * nw_ref[:, cols])).astype(BF16)

    def chunk_body(c, carry):
        rows = pl.ds(pl.multiple_of(c * CHUNK, CHUNK), CHUNK)

        gi = proj_ref[rows, H_GI:H_GI + LANES] + gbias_ref[0:1, :]
        gf = proj_ref[rows, H_GF:H_GF + LANES] + gbias_ref[1:2, :]
        bc = _chunk_cumsum(tri3, _log_sigmoid(gf))
        r = gi - bc
        m_prev = mst_ref[0:1, :]
        mu = jnp.maximum(m_prev, _token_cummax(r))
        g_row = bc[CHUNK - 1:CHUNK, :]
        m_new = jnp.maximum(g_row + m_prev, jnp.max(g_row + r, axis=0, keepdims=True))
        s_old = jnp.exp(g_row + m_prev - m_new)
        wt = jnp.exp(g_row + r - m_new).T
        rt = (r - m_prev).T
        mst_ref[0:1, :] = m_new
        packed = jnp.where(lane % N_REP < MLSTM_HEADS, mu - m_prev, bc + mu)
        rep_ref[...] = _lane_replicate(packed, sel_ref[...], N_REP)

        mq = (proj_ref[rows, H_MQ:H_MQ + QK_ALL] * 0.125).astype(BF16)
        mkt_f = proj_ref[rows, H_MK:H_MK + QK_ALL].T
        qhs = [jnp.where(lane_qk // HEAD_QK == h, mq, jnp.zeros_like(mq)) for h in range(MLSTM_HEADS)]
        scores = jnp.dot(jnp.concatenate(qhs, axis=0), mkt_f.astype(BF16), preferred_element_type=F32)
        for h in range(MLSTM_HEADS):
            hrows = slice(h * CHUNK, (h + 1) * CHUNK)
            alpha_b = rep_ref[:, h * LANES:(h + 1) * LANES]
            beta_b = rep_ref[:, (MLSTM_HEADS + h) * LANES:(MLSTM_HEADS + h + 1) * LANES]
            c_prev = cst_ref[h]
            vext = jnp.concatenate(
                [proj_ref[rows, H_MV + h * HEAD_V:H_MV + (h + 1) * HEAD_V].astype(BF16), ones_v], axis=1)

            dmat = jnp.exp(jnp.where(causal, rt[h:h + 1, :] - alpha_b, neg_inf))
            s = scores[hrows] * dmat
            kw = (mkt_f[h * HEAD_QK:(h + 1) * HEAD_QK] * wt[h:h + 1, :]).astype(BF16)
            both = jnp.dot(jnp.concatenate([s.astype(BF16), kw], axis=0), vext, preferred_element_type=F32)
            inter = jnp.dot(qhs[h], _embed_rows(c_prev.astype(BF16), [h * HEAD_QK]),
                            preferred_element_type=F32)
            si_b = jnp.exp(-alpha_b)
            num_ext = both[:CHUNK] + jnp.concatenate([si_b, si_b], axis=1) * inter
            x_ref[hrows, :] = num_ext[:, :HEAD_V]
            dn_ref[hrows, :] = jnp.maximum(jnp.abs(num_ext[:, HEAD_V:]), jnp.exp(-beta_b))
            cst_ref[h] = s_old[:, h:h + 1] * c_prev + both[CHUNK:]

        cs, sn = cos_ref[rows, :], sin_ref[rows, :]
        q1, q2 = proj_ref[rows, H_RQ:H_RQ + LANES], proj_ref[rows, H_RQ + LANES:H_RQ + 2 * LANES]
        k1, k2 = proj_ref[rows, H_RK:H_RK + LANES], proj_ref[rows, H_RK + LANES:H_RK + 2 * LANES]
        rq = (jnp.concatenate([q1 * cs - q2 * sn, q1 * sn + q2 * cs], axis=1) * 0.125).astype(BF16)
        rkt_f = jnp.concatenate([k1 * cs - k2 * sn, k1 * sn + k2 * cs], axis=1).T
        qhs = [jnp.where((lane_qk % LANES) // half == h, rq, jnp.zeros_like(rq)) for h in range(RET_HEADS)]
        scores = jnp.dot(jnp.concatenate(qhs, axis=0), rkt_f.astype(BF16), preferred_element_type=F32)
        for h in range(RET_HEADS):
            hrows = slice((MLSTM_HEADS + h) * CHUNK, (MLSTM_HEADS + h + 1) * CHUNK)
            v = proj_ref[rows, H_RV + h * HEAD_V:H_RV + (h + 1) * HEAD_V].astype(BF16)
            r_prev = rst_ref[h]
            s = scores[h * CHUNK:(h + 1) * CHUNK] * dec_ref[h]
            k_own = jnp.concatenate([rkt_f[h * half:(h + 1) * half], rkt_f[LANES + h * half:LANES + (h + 1) * half]],
                                    axis=0)
            kw = (k_own * rrow_ref[h:h + 1, :]).astype(BF16)
            both = jnp.dot(jnp.concatenate([s.astype(BF16), kw], axis=0), v, preferred_element_type=F32)
            inter = jnp.dot(qhs[h], _embed_rows(r_prev.astype(BF16), [h * half, LANES + h * half]),
                            preferred_element_type=F32)
            x_ref[hrows, :] = both[:CHUNK] + wq_ref[h] * inter
            rst_ref[h] = rrow_ref[RET_HEADS + h:RET_HEADS + h + 1, 0:1] * r_prev + both[CHUNK:]

        finish_chunk(rows)
        return carry

    lax.fori_loop(0, n_chunks, chunk_body, 0)
    o_ref[...] = h_ref[...] + jnp.dot(ycat_ref[...], wout_ref[...], preferred_element_type=F32)


def _hyb_core(proj, h, cos_t, sin_t, gbias, nw_row, dec, wq_rep, rrow, wout):
    b, s, d = h.shape
    ts = TS_CORE
    inner = wout.shape[0]
    n_heads = MLSTM_HEADS + RET_HEADS
    return pl.pallas_call(
        _hyb_core_kernel,
        out_shape=jax.ShapeDtypeStruct((b, s, d), F32),
        grid=(b, s // ts),
        in_specs=[
            pl.BlockSpec((None, ts, H_PROJ), lambda bi, i: (bi, i, 0)),
            pl.BlockSpec((None, ts, d), lambda bi, i: (bi, i, 0)),
            pl.BlockSpec((ts, LANES), lambda bi, i: (i, 0)),
            pl.BlockSpec((ts, LANES), lambda bi, i: (i, 0)),
            _resident((SUBLANES, LANES)),
            _resident((1, inner)),
            _resident((RET_HEADS, CHUNK, CHUNK)),
            _resident((RET_HEADS, CHUNK, HEAD_V)),
            _resident((SUBLANES, LANES)),
            _resident((LANES, N_REP * LANES)),
            _resident((inner, d)),
        ],
        out_specs=pl.BlockSpec((None, ts, d), lambda bi, i: (bi, i, 0)),
        scratch_shapes=[
            pltpu.VMEM((MLSTM_HEADS, HEAD_QK, 2 * HEAD_V), F32),
            pltpu.VMEM((SUBLANES, LANES), F32),
            pltpu.VMEM((RET_HEADS, HEAD_QK, HEAD_V), F32),
            pltpu.VMEM((CHUNK, N_REP * LANES), F32),
            pltpu.VMEM((n_heads * CHUNK, HEAD_V), F32),
            pltpu.VMEM((MLSTM_HEADS * CHUNK, HEAD_V), F32),
            pltpu.VMEM((ts, inner), BF16),
        ],
        compiler_params=_compiler_params(("parallel", "arbitrary")),
        name="hybrid_core",
    )(proj, h, cos_t, sin_t, gbias, nw_row, dec, wq_rep, rrow, _replicate_selector(N_REP), wout)


def _ssd_core_kernel(proj_ref, h_ref, dtb_ref, alog_ref, dsk_ref, nw_ref, sel_ref, wout_ref, o_ref,
                     st_ref, csb_ref, lhs_ref, rhs_ref, stl_ref, ycat_ref):
    @pl.when(pl.program_id(1) == 0)
    def _init():
        st_ref[...] = jnp.zeros_like(st_ref)

    causal = _causal_mask()
    tri3 = jnp.tile(causal.astype(BF16), (1, 3))
    lane_g = lax.broadcasted_iota(jnp.int32, (1, SSD_GW), 1)
    neg_inf = jnp.float32(-jnp.inf)
    a_row = -jnp.exp(alog_ref[...])
    n_chunks = proj_ref.shape[0] // CHUNK

    def chunk_body(c, carry):
        rows = pl.ds(pl.multiple_of(c * CHUNK, CHUNK), CHUNK)

        dtp = _softplus(proj_ref[rows, S_DT:S_DT + LANES] + dtb_ref[...])
        cs = _chunk_cumsum(tri3, dtp * a_row)
        cst = cs.T
        dtt = dtp.T
        csb_ref[...] = _lane_replicate(cs, sel_ref[...], SSD_HEADS)
        toendt = jnp.exp(cst[:, CHUNK - 1:CHUNK] - cst) * dtt
        dec_last = jnp.exp(cs[CHUNK - 1:CHUNK, :])

        for g in range(SSD_GROUPS):
            gcols = slice(g * SSD_GW, (g + 1) * SSD_GW)
            bg = proj_ref[rows, S_B + g * SSD_N:S_B + (g + 1) * SSD_N]
            cg = proj_ref[rows, S_C + g * SSD_N:S_C + (g + 1) * SSD_N]
            xg = proj_ref[rows, S_X + g * SSD_GW:S_X + (g + 1) * SSD_GW]
            cbm = lax.dot_general(cg.astype(BF16), bg.astype(BF16), (((1,), (1,)), ((), ())),
                                  preferred_element_type=F32)
            bgt = bg.T
            xb = xg.astype(BF16)
            st_prev = st_ref[g]
            hb = st_prev.astype(BF16)
            decay_row = jnp.zeros((1, SSD_GW), F32)
            for j in range(SSD_REP):
                hh = g * SSD_REP + j
                jmask = lane_g // SSD_P == j
                csb = csb_ref[:, hh * LANES:(hh + 1) * LANES]
                seg = jnp.exp(jnp.where(causal, csb - cst[hh:hh + 1, :], neg_inf))
                lhs_ref[g, :, j * CHUNK:(j + 1) * CHUNK] = (cbm * seg * dtt[hh:hh + 1, :]).astype(BF16)
                lhs_ref[g, :, (SSD_REP + j) * SSD_N:(SSD_REP + j + 1) * SSD_N] = (cg * jnp.exp(csb)).astype(BF16)
                rhs_ref[g, j * CHUNK:(j + 1) * CHUNK, :] = jnp.where(jmask, xb, jnp.zeros_like(xb))
                rhs_ref[g, (SSD_REP + j) * SSD_N:(SSD_REP + j + 1) * SSD_N, :] = jnp.where(jmask, hb, jnp.zeros_like(hb))
                stl_ref[g, :, j * CHUNK:(j + 1) * CHUNK] = (bgt * toendt[hh:hh + 1, :]).astype(BF16)
                decay_row = jnp.where(jmask, dec_last[:, hh:hh + 1], decay_row)
            yg = dsk_ref[:, gcols] * xg + jnp.dot(lhs_ref[g], rhs_ref[g], preferred_element_type=F32)
            st_ref[g] = st_prev * decay_row + jnp.dot(stl_ref[g], rhs_ref[g, :SSD_REP * CHUNK, :],
                                                      preferred_element_type=F32)
            yz = yg * proj_ref[rows, S_Z + g * SSD_GW:S_Z + (g + 1) * SSD_GW]
            ycat_ref[rows, gcols] = _rmsnorm(yz, nw_ref[:, gcols], 1e-5).astype(BF16)
        return carry

    lax.fori_loop(0, n_chunks, chunk_body, 0)
    o_ref[...] = h_ref[...] + jnp.dot(ycat_ref[...], wout_ref[...], preferred_element_type=F32)


def _ssd_core(proj, h, dtb, alog, dsk, nw_row, wout):
    b, s, d = h.shape
    ts = TS_CORE
    return pl.pallas_call(
        _ssd_core_kernel,
        out_shape=jax.ShapeDtypeStruct((b, s, d), F32),
        grid=(b, s // ts),
        in_specs=[
            pl.BlockSpec((None, ts, S_PROJ), lambda bi, i: (bi, i, 0)),
            pl.BlockSpec((None, ts, d), lambda bi, i: (bi, i, 0)),
            _resident((1, LANES)),
            _resident((1, LANES)),
            _resident((1, SSD_INNER)),
            _resident((1, SSD_INNER)),
            _resident((LANES, SSD_HEADS * LANES)),
            _resident((SSD_INNER, d)),
        ],
        out_specs=pl.BlockSpec((None, ts, d), lambda bi, i: (bi, i, 0)),
        scratch_shapes=[
            pltpu.VMEM((SSD_GROUPS, SSD_N, SSD_GW), F32),
            pltpu.VMEM((CHUNK, SSD_HEADS * LANES), F32),
            pltpu.VMEM((SSD_GROUPS, CHUNK, 2 * SSD_REP * CHUNK), BF16),
            pltpu.VMEM((SSD_GROUPS, 2 * SSD_REP * CHUNK, SSD_GW), BF16),
            pltpu.VMEM((SSD_GROUPS, CHUNK, SSD_REP * CHUNK), BF16),
            pltpu.VMEM((ts, SSD_INNER), BF16),
        ],
        compiler_params=_compiler_params(("parallel", "arbitrary")),
        name="ssd_core",
    )(proj, h, dtb, alog, dsk, nw_row, _replicate_selector(SSD_HEADS), wout)


def _replicate_selector(n_cols):
    row = jnp.arange(LANES)[:, None]
    sel = (row % n_cols == jnp.arange(n_cols * LANES)[None, :] // LANES) & (row < 3 * n_cols)
    return sel.astype(BF16)


def _pad_lanes(row, width=LANES):
    return jnp.pad(row, ((0, 0), (0, width - row.shape[1])))


def _hybrid_layer(h, nw, w_in, i_bias, f_bias, m_norm_w, r_norm_w, w_out, cos_t, sin_t, dec, wq_rep, rrow):
    b, s, d = h.shape
    mq, mk, mv, mi, mf, mo, rq, rk, rv, rg = jnp.split(
        w_in, (256, 512, 1024, 1028, 1032, 1544, 1800, 2056, 2568), axis=1)
    perm = jnp.concatenate([
        (jnp.arange(LANES) // 32) * 64 + jnp.arange(LANES) % 32,
        (jnp.arange(LANES) // 32) * 64 + 32 + jnp.arange(LANES) % 32])
    gate_cols = lambda g: _pad_lanes(jnp.tile(g, (1, 3 * N_REP // MLSTM_HEADS)))
    w_cat = jnp.concatenate([mq, mk, mv, mo, rq[:, perm], rk[:, perm], rv, rg, gate_cols(mi), gate_cols(mf)],
                            axis=1).astype(BF16)
    gbias = jnp.pad(jnp.concatenate([gate_cols(i_bias[None, :]), gate_cols(f_bias[None, :])], axis=0),
                    ((0, SUBLANES - 2), (0, 0)))
    proj = _norm_matmul(h.reshape(b * s, d), nw[None, :], w_cat).reshape(b, s, H_PROJ)
    nw_row = jnp.concatenate([m_norm_w, r_norm_w])[None, :]
    return _hyb_core(proj, h, cos_t, sin_t, gbias, nw_row, dec, wq_rep, rrow, w_out.astype(BF16))


def _ssd_layer(h, nw, w_in, conv_w, conv_b, dt_bias, a_log, d_skip, norm_w, w_out):
    b, s, d = h.shape
    thrice = lambda a: _pad_lanes(jnp.tile(a, (1, 3)))
    w_cat = jnp.concatenate([w_in[:, :S_DT], thrice(w_in[:, S_DT:])], axis=1).astype(BF16)
    proj = _ssd_proj(h, nw[None, :], w_cat, conv_w, conv_b[None, :])
    dsk = jnp.repeat(d_skip, SSD_P)[None, :]
    return _ssd_core(proj, h, thrice(dt_bias[None, :]), thrice(a_log[None, :]), dsk, norm_w[None, :],
                     w_out.astype(BF16))


def _position_tables(seq):
    dk = 64
    inv = ROPE_BASE ** (-jnp.arange(0, dk, 2, dtype=F32) / dk)
    ang = jnp.arange(seq, dtype=F32)[:, None] * inv[None, :]
    cos_t = jnp.tile(jnp.cos(ang), (1, RET_HEADS))
    sin_t = jnp.tile(jnp.sin(ang), (1, RET_HEADS))
    log_gamma = jnp.log(1.0 - 2.0 ** (-5.0 - jnp.arange(RET_HEADS, dtype=F32)))
    idx = jnp.arange(CHUNK, dtype=F32)
    rel = idx[:, None] - idx[None, :]
    dec = jnp.where((rel >= 0)[None], jnp.exp(jnp.maximum(rel, 0.0)[None] * log_gamma[:, None, None]), 0.0)
    w_q = jnp.exp((idx + 1.0)[:, None] * log_gamma)
    w_k = jnp.exp((CHUNK - 1.0 - idx)[:, None] * log_gamma)
    chunk_decay = jnp.broadcast_to(jnp.exp(CHUNK * log_gamma)[:, None], (RET_HEADS, CHUNK))
    tok = _token_of_row(jnp.arange(CHUNK))
    wq_rep = jnp.broadcast_to(w_q[tok].T[:, :, None], (RET_HEADS, CHUNK, HEAD_V))
    rrow = jnp.concatenate([w_k[tok].T, chunk_decay], axis=0)
    return _interleave_tokens(cos_t, 0), _interleave_tokens(sin_t, 0), dec[:, tok][:, :, tok], wq_rep, rrow


def _interleave_tokens(a, axis):
    shape = a.shape
    split = shape[:axis] + (shape[axis] // CHUNK, SUBLANES, CHUNK // SUBLANES) + shape[axis + 1:]
    return jnp.swapaxes(a.reshape(split), axis + 1, axis + 2).reshape(shape)


def _deinterleave_tokens(a, axis):
    shape = a.shape
    split = shape[:axis] + (shape[axis] // CHUNK, CHUNK // SUBLANES, SUBLANES) + shape[axis + 1:]
    return jnp.swapaxes(a.reshape(split), axis + 1, axis + 2).reshape(shape)


@jax.jit
def kernel(x, norm_mix_w, norm_mlp_w, hyb_w_in, mlstm_i_bias, mlstm_f_bias, mlstm_norm_w, ret_norm_w, hyb_w_out,
           ssd_w_in, ssd_conv_w, ssd_conv_b, ssd_dt_bias, ssd_a_log, ssd_d, ssd_norm_w, ssd_w_out, mlp_w1, mlp_w2,
           final_norm_w):
    b, s, d = x.shape
    depth = norm_mix_w.shape[0]
    cos_t, sin_t, dec, wq_rep, rrow = _position_tables(s)
    h = _interleave_tokens(x, 1)
    for layer in range(depth):
        j = layer // 2
        if layer % 2 == 0:
            h = _hybrid_layer(h, norm_mix_w[layer], hyb_w_in[j], mlstm_i_bias[j], mlstm_f_bias[j], mlstm_norm_w[j],
                              ret_norm_w[j], hyb_w_out[j], cos_t, sin_t, dec, wq_rep, rrow)
        else:
            h = _ssd_layer(h, norm_mix_w[layer], ssd_w_in[j], ssd_conv_w[j], ssd_conv_b[j], ssd_dt_bias[j],
                           ssd_a_log[j], ssd_d[j], ssd_norm_w[j], ssd_w_out[j])
        h = _mlp(h.reshape(b * s, d), norm_mlp_w[layer][None, :], mlp_w1[layer].astype(BF16),
                 mlp_w2[layer].astype(BF16), final_norm_w[None, :], layer == depth - 1).reshape(b, s, d)
    return _deinterleave_tokens(h, 1)
```

```python
import functools

import jax
import jax.numpy as jnp
from jax import lax
from jax.experimental import pallas as pl
from jax.experimental.pallas import tpu as pltpu

F32 = jnp.float32
BF16 = jnp.bfloat16

CHUNK = 128
LANES = 128
SUBLANES = 8
VMEM_LIMIT_BYTES = 56 * 1024 * 1024
ROPE_BASE = 10000.0

MLSTM_HEADS = 4
RET_HEADS = 4
HEAD_QK = 64
QK_ALL = 256
HEAD_V = 128
SSD_GROUPS = 8
SSD_REP = 4
SSD_P = 64
SSD_N = 128
SSD_HEADS = SSD_GROUPS * SSD_REP
SSD_GW = SSD_REP * SSD_P
SSD_INNER = SSD_GROUPS * SSD_GW
SSD_CONV = 4

H_MQ, H_MK, H_MV, H_MO = 0, 256, 512, 1024
H_RQ, H_RK, H_RV, H_RG = 1536, 1792, 2048, 2560
H_GI, H_GF = 3072, 3200
H_PROJ = H_GF + LANES
S_Z, S_X, S_B, S_C, S_DT = 0, 2048, 4096, 5120, 6144
S_PROJ = S_DT + LANES
S_CONV_DIM = S_DT - S_X

TM_PROJ = 512
TM_SSD_PROJ = 256
CONV_COLS = 512
HALO_SLABS = SSD_CONV - 1
TM_MLP = 512
TS_CORE = 512
FF_CHUNK = 1024


def _log1p_exp_neg_abs(x):
    return jnp.log1p(jnp.exp(-jnp.abs(x)))


def _softplus(x):
    return jnp.maximum(x, 0.0) + _log1p_exp_neg_abs(x)


def _log_sigmoid(x):
    return jnp.minimum(x, 0.0) - _log1p_exp_neg_abs(x)


def _sigmoid(x):
    return 0.5 * jnp.tanh(0.5 * x) + 0.5


def _silu(x):
    half = 0.5 * x
    return half * jnp.tanh(half) + half


def _rmsnorm(x, w_row, eps):
    return x * lax.rsqrt(jnp.mean(x * x, axis=-1, keepdims=True) + eps) * w_row


def _token_of_row(r):
    return (r % SUBLANES) * (CHUNK // SUBLANES) + r // SUBLANES


def _causal_mask():
    row = lax.broadcasted_iota(jnp.int32, (CHUNK, CHUNK), 0)
    col = lax.broadcasted_iota(jnp.int32, (CHUNK, CHUNK), 1)
    return _token_of_row(row) >= _token_of_row(col)


def _split3(x):
    hi = x.astype(BF16)
    r1 = x - hi.astype(F32)
    mid = r1.astype(BF16)
    lo = (r1 - mid.astype(F32)).astype(BF16)
    return hi, mid, lo


def _chunk_cumsum(tri3, x):
    return jnp.dot(tri3, jnp.concatenate(_split3(x), axis=0), preferred_element_type=F32)


def _lane_replicate(x, c):
    return jnp.broadcast_to(x[:, c:c + 1], x.shape)


def _compiler_params(semantics):
    return pltpu.CompilerParams(dimension_semantics=semantics, vmem_limit_bytes=VMEM_LIMIT_BYTES)


def _resident(shape):
    zeros = (0,) * len(shape)
    return pl.BlockSpec(shape, lambda *_: zeros, pipeline_mode=pl.Buffered(1))


def _norm_matmul_kernel(x_ref, nw_ref, w_ref, o_ref):
    xn = _rmsnorm(x_ref[...], nw_ref[...], 1e-6).astype(BF16)
    o_ref[...] = jnp.dot(xn, w_ref[...], preferred_element_type=F32)


def _norm_matmul(x2d, nw_row, w_bf16):
    t, d = x2d.shape
    n = w_bf16.shape[1]
    return pl.pallas_call(
        _norm_matmul_kernel,
        out_shape=jax.ShapeDtypeStruct((t, n), F32),
        grid=(t // TM_PROJ,),
        in_specs=[pl.BlockSpec((TM_PROJ, d), lambda i: (i, 0)), _resident((1, d)), _resident((d, n))],
        out_specs=pl.BlockSpec((TM_PROJ, n), lambda i: (i, 0)),
        compiler_params=_compiler_params(("parallel",)),
        name="norm_matmul",
    )(x2d, nw_row, w_bf16)


def _causal_conv_silu(raw, prev_tail, cw_ref, cb_ref, cc, last_sublane):
    wrapped = []
    for k in range(HALO_SLABS):
        cur = raw[CHUNK - (HALO_SLABS - k) * SUBLANES:CHUNK - (HALO_SLABS - k - 1) * SUBLANES]
        prv = prev_tail[k * SUBLANES:(k + 1) * SUBLANES]
        wrapped.append(pltpu.roll(jnp.where(last_sublane, prv, cur), 1, 0))
    acc = raw * cw_ref[SSD_CONV - 1:SSD_CONV, cc] + cb_ref[:, cc]
    for j in range(1, SSD_CONV):
        shifted = jnp.concatenate(wrapped[HALO_SLABS - j:] + [raw[:CHUNK - j * SUBLANES]], axis=0)
        acc = acc + shifted * cw_ref[SSD_CONV - 1 - j:SSD_CONV - j, cc]
    return _silu(acc)


def _ssd_proj_kernel(x_ref, nw_ref, w_ref, cw_ref, cb_ref, o_ref, halo_ref):
    @pl.when(pl.program_id(1) == 0)
    def _init():
        halo_ref[...] = jnp.zeros_like(halo_ref)

    xn = _rmsnorm(x_ref[...], nw_ref[...], 1e-6).astype(BF16)
    last_sublane = lax.broadcasted_iota(jnp.int32, (SUBLANES, CONV_COLS), 0) == SUBLANES - 1
    n_chunks = x_ref.shape[0] // CHUNK
    tail = slice(CHUNK - HALO_SLABS * SUBLANES, CHUNK)

    for cb in range(S_X // CONV_COLS):
        cols = slice(cb * CONV_COLS, (cb + 1) * CONV_COLS)
        o_ref[:, cols] = _silu(jnp.dot(xn, w_ref[:, cols], preferred_element_type=F32))
    o_ref[:, S_DT:] = jnp.dot(xn, w_ref[:, S_DT:], preferred_element_type=F32)
    for cb in range(S_CONV_DIM // CONV_COLS):
        cc = slice(cb * CONV_COLS, (cb + 1) * CONV_COLS)
        pc = slice(S_X + cb * CONV_COLS, S_X + (cb + 1) * CONV_COLS)
        raw_all = jnp.dot(xn, w_ref[:, pc], preferred_element_type=F32)
        prev_tail = halo_ref[:, cc]
        for c in range(n_chunks):
            raw = raw_all[c * CHUNK:(c + 1) * CHUNK]
            o_ref[c * CHUNK:(c + 1) * CHUNK, pc] = _causal_conv_silu(raw, prev_tail, cw_ref, cb_ref, cc, last_sublane)
            prev_tail = raw[tail]
        halo_ref[:, cc] = prev_tail


def _ssd_proj(h, nw_row, w_bf16, cw, cb_row):
    b, s, d = h.shape
    tm = TM_SSD_PROJ
    return pl.pallas_call(
        _ssd_proj_kernel,
        out_shape=jax.ShapeDtypeStruct((b, s, S_PROJ), F32),
        grid=(b, s // tm),
        in_specs=[pl.BlockSpec((None, tm, d), lambda bi, i: (bi, i, 0)), _resident((1, d)), _resident((d, S_PROJ)),
                  _resident((SSD_CONV, S_CONV_DIM)), _resident((1, S_CONV_DIM))],
        out_specs=pl.BlockSpec((None, tm, S_PROJ), lambda bi, i: (bi, i, 0)),
        scratch_shapes=[pltpu.VMEM((HALO_SLABS * SUBLANES, S_CONV_DIM), F32)],
        compiler_params=_compiler_params(("parallel", "arbitrary")),
        name="ssd_proj",
    )(h, nw_row, w_bf16, cw, cb_row)


def _mlp_kernel(x_ref, nw_ref, w1_ref, w2_ref, fnw_ref, o_ref, *, final):
    x = x_ref[...]
    xn = _rmsnorm(x, nw_ref[...], 1e-6).astype(BF16)
    acc = x
    for f in range(w1_ref.shape[1] // FF_CHUNK):
        cols = slice(f * FF_CHUNK, (f + 1) * FF_CHUNK)
        hid = jnp.maximum(jnp.dot(xn, w1_ref[:, cols], preferred_element_type=F32), 0.0)
        acc = acc + jnp.dot((hid * hid).astype(BF16), w2_ref[cols, :], preferred_element_type=F32)
    if final:
        acc = _rmsnorm(acc, fnw_ref[...], 1e-6)
    o_ref[...] = acc


def _mlp(x2d, nw_row, w1, w2, fnw_row, final):
    t, d = x2d.shape
    dff = w1.shape[1]
    return pl.pallas_call(
        functools.partial(_mlp_kernel, final=final),
        out_shape=jax.ShapeDtypeStruct((t, d), F32),
        grid=(t // TM_MLP,),
        in_specs=[pl.BlockSpec((TM_MLP, d), lambda i: (i, 0)), _resident((1, d)), _resident((d, dff)),
                  _resident((dff, d)), _resident((1, d))],
        out_specs=pl.BlockSpec((TM_MLP, d), lambda i: (i, 0)),
        compiler_params=_compiler_params(("parallel",)),
        name="mlp",
    )(x2d, nw_row, w1, w2, fnw_row)


def _shift_tokens(x, d, fill):
    n_slab = CHUNK // SUBLANES
    sub = lax.broadcasted_iota(jnp.int32, (SUBLANES, x.shape[1]), 0)
    slabs = [x[i * SUBLANES:(i + 1) * SUBLANES] for i in range(n_slab)]
    if d < n_slab:
        wrapped = [jnp.where(sub == 0, fill, pltpu.roll(slabs[n_slab - d + i], 1, 0)) for i in range(d)]
        return jnp.concatenate(wrapped + slabs[:n_slab - d], axis=0)
    k = d // n_slab
    return jnp.concatenate([jnp.where(sub < k, fill, pltpu.roll(sl, k, 0)) for sl in slabs], axis=0)


def _token_cummax(x):
    d = 1
    while d < CHUNK:
        x = jnp.maximum(x, _shift_tokens(x, d, -jnp.inf))
        d *= 2
    return x


def _split2(x):
    hi = x.astype(BF16)
    return hi, (x - hi.astype(F32)).astype(BF16)


def _row_mean_replicated(x, j2):
    return jnp.dot(jnp.concatenate(_split2(x), axis=1), j2, preferred_element_type=F32)


def _embed_rows(block, starts):
    piece = block.shape[0] // len(starts)
    parts, pos = [], 0
    for i, start in enumerate(starts):
        if start > pos:
            parts.append(jnp.zeros((start - pos, block.shape[1]), block.dtype))
        parts.append(block[i * piece:(i + 1) * piece])
        pos = start + piece
    if pos < QK_ALL:
        parts.append(jnp.zeros((QK_ALL - pos, block.shape[1]), block.dtype))
    return jnp.concatenate(parts, axis=0)


def _hyb_core_kernel(proj_ref, h_ref, cos_ref, sin_ref, gbias_ref, nw_ref, dec_ref, wq_ref, rrow_ref,
                     wout_ref, o_ref, cst_ref, mst_ref, rst_ref, x_ref, dn_ref, ycat_ref):
    @pl.when(pl.program_id(1) == 0)
    def _init():
        cst_ref[...] = jnp.zeros_like(cst_ref)
        mst_ref[...] = jnp.zeros_like(mst_ref)
        rst_ref[...] = jnp.zeros_like(rst_ref)

    causal = _causal_mask()
    tri3 = jnp.tile(causal.astype(BF16), (1, 3))
    lane_qk = lax.broadcasted_iota(jnp.int32, (1, QK_ALL), 1)
    ones_v = jnp.ones((CHUNK, HEAD_V), BF16)
    j2 = jnp.full((2 * HEAD_V, HEAD_V), 1.0 / HEAD_V, BF16)
    neg_inf = jnp.float32(-jnp.inf)
    n_chunks = proj_ref.shape[0] // CHUNK
    n_heads = MLSTM_HEADS + RET_HEADS
    half = HEAD_QK // 2

    def finish_chunk(rows):
        x = x_ref[...]
        d = x - _row_mean_replicated(x, j2)
        var = _row_mean_replicated(d * d, j2)
        dn = dn_ref[...]
        eps = jnp.concatenate([1e-5 * dn * dn, jnp.full((RET_HEADS * CHUNK, HEAD_V), 1e-5, F32)], axis=0)
        y = d * lax.rsqrt(var + eps)
        for h in range(n_heads):
            cols = slice(h * HEAD_V, (h + 1) * HEAD_V)
            if h < MLSTM_HEADS:
                gate = _sigmoid(proj_ref[rows, H_MO + h * HEAD_V:H_MO + (h + 1) * HEAD_V])
            else:
                hr = h - MLSTM_HEADS
                gate = _silu(proj_ref[rows, H_RG + hr * HEAD_V:H_RG + (hr + 1) * HEAD_V])
            ycat_ref[rows, cols] = (gate * (y[h * CHUNK:(h + 1) * CHUNK] * nw_ref[:, cols])).astype(BF16)

    def chunk_body(c, carry):
        rows = pl.ds(pl.multiple_of(c * CHUNK, CHUNK), CHUNK)

        gi = proj_ref[rows, H_GI:H_GI + LANES] + gbias_ref[0:1, :]
        gf = proj_ref[rows, H_GF:H_GF + LANES] + gbias_ref[1:2, :]
        bc = _chunk_cumsum(tri3, _log_sigmoid(gf))
        r = gi - bc
        m_prev = mst_ref[0:1, :]
        mu = jnp.maximum(m_prev, _token_cummax(r))
        g_row = bc[CHUNK - 1:CHUNK, :]
        m_new = jnp.maximum(g_row + m_prev, jnp.max(g_row + r, axis=0, keepdims=True))
        s_old = jnp.exp(g_row + m_prev - m_new)
        wt = jnp.exp(g_row + r - m_new).T
        rt = (r - m_prev).T
        mst_ref[0:1, :] = m_new
        alpha = mu - m_prev
        beta = bc + mu

        mq = (proj_ref[rows, H_MQ:H_MQ + QK_ALL] * 0.125).astype(BF16)
        mkt_f = proj_ref[rows, H_MK:H_MK + QK_ALL].T
        qhs = [jnp.where(lane_qk // HEAD_QK == h, mq, jnp.zeros_like(mq)) for h in range(MLSTM_HEADS)]
        scores = jnp.dot(jnp.concatenate(qhs, axis=0), mkt_f.astype(BF16), preferred_element_type=F32)
        for h in range(MLSTM_HEADS):
            hrows = slice(h * CHUNK, (h + 1) * CHUNK)
            alpha_b = _lane_replicate(alpha, h)
            beta_b = _lane_replicate(beta, h)
            c_prev = cst_ref[h]
            vext = jnp.concatenate(
                [proj_ref[rows, H_MV + h * HEAD_V:H_MV + (h + 1) * HEAD_V].astype(BF16), ones_v], axis=1)

            dmat = jnp.exp(jnp.where(causal, rt[h:h + 1, :] - alpha_b, neg_inf))
            s = scores[hrows] * dmat
            kw = (mkt_f[h * HEAD_QK:(h + 1) * HEAD_QK] * wt[h:h + 1, :]).astype(BF16)
            both = jnp.dot(jnp.concatenate([s.astype(BF16), kw], axis=0), vext, preferred_element_type=F32)
            inter = jnp.dot(qhs[h], _embed_rows(c_prev.astype(BF16), [h * HEAD_QK]),
                            preferred_element_type=F32)
            si_b = jnp.exp(-alpha_b)
            num_ext = both[:CHUNK] + jnp.concatenate([si_b, si_b], axis=1) * inter
            x_ref[hrows, :] = num_ext[:, :HEAD_V]
            dn_ref[hrows, :] = jnp.maximum(jnp.abs(num_ext[:, HEAD_V:]), jnp.exp(-beta_b))
            cst_ref[h] = s_old[:, h:h + 1] * c_prev + both[CHUNK:]

        cs, sn = cos_ref[rows, :], sin_ref[rows, :]
        q1, q2 = proj_ref[rows, H_RQ:H_RQ + LANES], proj_ref[rows, H_RQ + LANES:H_RQ + 2 * LANES]
        k1, k2 = proj_ref[rows, H_RK:H_RK + LANES], proj_ref[rows, H_RK + LANES:H_RK + 2 * LANES]
        rq = (jnp.concatenate([q1 * cs - q2 * sn, q1 * sn + q2 * cs], axis=1) * 0.125).astype(BF16)
        rkt_f = jnp.concatenate([k1 * cs - k2 * sn, k1 * sn + k2 * cs], axis=1).T
        qhs = [jnp.where((lane_qk % LANES) // half == h, rq, jnp.zeros_like(rq)) for h in range(RET_HEADS)]
        scores = jnp.dot(jnp.concatenate(qhs, axis=0), rkt_f.astype(BF16), preferred_element_type=F32)
        for h in range(RET_HEADS):
            hrows = slice((MLSTM_HEADS + h) * CHUNK, (MLSTM_HEADS + h + 1) * CHUNK)
            v = proj_ref[rows, H_RV + h * HEAD_V:H_RV + (h + 1) * HEAD_V].astype(BF16)
            r_prev = rst_ref[h]
            s = scores[h * CHUNK:(h + 1) * CHUNK] * dec_ref[h]
            k_own = jnp.concatenate([rkt_f[h * half:(h + 1) * half], rkt_f[LANES + h * half:LANES + (h + 1) * half]],
                                    axis=0)
            kw = (k_own * rrow_ref[h:h + 1, :]).astype(BF16)
            both = jnp.dot(jnp.concatenate([s.astype(BF16), kw], axis=0), v, preferred_element_type=F32)
            inter = jnp.dot(qhs[h], _embed_rows(r_prev.astype(BF16), [h * half, LANES + h * half]),
                            preferred_element_type=F32)
            x_ref[hrows, :] = both[:CHUNK] + wq_ref[h] * inter
            rst_ref[h] = rrow_ref[RET_HEADS + h:RET_HEADS + h + 1, 0:1] * r_prev + both[CHUNK:]

        finish_chunk(rows)
        return carry

    lax.fori_loop(0, n_chunks, chunk_body, 0)
    o_ref[...] = h_ref[...] + jnp.dot(ycat_ref[...], wout_ref[...], preferred_element_type=F32)


def _hyb_core(proj, h, cos_t, sin_t, gbias, nw_row, dec, wq_rep, rrow, wout):
    b, s, d = h.shape
    ts = TS_CORE
    inner = wout.shape[0]
    n_heads = MLSTM_HEADS + RET_HEADS
    return pl.pallas_call(
        _hyb_core_kernel,
        out_shape=jax.ShapeDtypeStruct((b, s, d), F32),
        grid=(b, s // ts),
        in_specs=[
            pl.BlockSpec((None, ts, H_PROJ), lambda bi, i: (bi, i, 0)),
            pl.BlockSpec((None, ts, d), lambda bi, i: (bi, i, 0)),
            pl.BlockSpec((ts, LANES), lambda bi, i: (i, 0)),
            pl.BlockSpec((ts, LANES), lambda bi, i: (i, 0)),
            _resident((SUBLANES, LANES)),
            _resident((1, inner)),
            _resident((RET_HEADS, CHUNK, CHUNK)),
            _resident((RET_HEADS, CHUNK, HEAD_V)),
            _resident((SUBLANES, LANES)),
            _resident((inner, d)),
        ],
        out_specs=pl.BlockSpec((None, ts, d), lambda bi, i: (bi, i, 0)),
        scratch_shapes=[
            pltpu.VMEM((MLSTM_HEADS, HEAD_QK, 2 * HEAD_V), F32),
            pltpu.VMEM((SUBLANES, LANES), F32),
            pltpu.VMEM((RET_HEADS, HEAD_QK, HEAD_V), F32),
            pltpu.VMEM((n_heads * CHUNK, HEAD_V), F32),
            pltpu.VMEM((MLSTM_HEADS * CHUNK, HEAD_V), F32),
            pltpu.VMEM((ts, inner), BF16),
        ],
        compiler_params=_compiler_params(("parallel", "arbitrary")),
        name="hybrid_core",
    )(proj, h, cos_t, sin_t, gbias, nw_row, dec, wq_rep, rrow, wout)


def _ssd_core_kernel(proj_ref, h_ref, dtb_ref, alog_ref, dsk_ref, nw_ref, wout_ref, o_ref, st_ref, ycat_ref):
    @pl.when(pl.program_id(1) == 0)
    def _init():
        st_ref[...] = jnp.zeros_like(st_ref)

    causal = _causal_mask()
    tri3 = jnp.tile(causal.astype(BF16), (1, 3))
    lane_g = lax.broadcasted_iota(jnp.int32, (1, SSD_GW), 1)
    neg_inf = jnp.float32(-jnp.inf)
    a_row = -jnp.exp(alog_ref[...])
    n_chunks = proj_ref.shape[0] // CHUNK

    def chunk_body(c, carry):
        rows = pl.ds(pl.multiple_of(c * CHUNK, CHUNK), CHUNK)

        dtp = _softplus(proj_ref[rows, S_DT:S_DT + LANES] + dtb_ref[...])
        cs = _chunk_cumsum(tri3, dtp * a_row)
        cst = cs.T
        dtt = dtp.T
        toendt = jnp.exp(cst[:, CHUNK - 1:CHUNK] - cst) * dtt
        dec_last = jnp.exp(cs[CHUNK - 1:CHUNK, :])

        for g in range(SSD_GROUPS):
            gcols = slice(g * SSD_GW, (g + 1) * SSD_GW)
            bg = proj_ref[rows, S_B + g * SSD_N:S_B + (g + 1) * SSD_N]
            cg = proj_ref[rows, S_C + g * SSD_N:S_C + (g + 1) * SSD_N]
            xg = proj_ref[rows, S_X + g * SSD_GW:S_X + (g + 1) * SSD_GW]
            cbm = lax.dot_general(cg.astype(BF16), bg.astype(BF16), (((1,), (1,)), ((), ())),
                                  preferred_element_type=F32)
            bgt = bg.T
            xb = xg.astype(BF16)
            st_prev = st_ref[g]
            hb = st_prev.astype(BF16)
            decay_row = jnp.zeros((1, SSD_GW), F32)
            lhs, rhs, st_lhs, st_rhs = [], [], [], []
            for j in range(SSD_REP):
                hh = g * SSD_REP + j
                jmask = lane_g // SSD_P == j
                xm = jnp.where(jmask, xb, jnp.zeros_like(xb))
                csb = _lane_replicate(cs, hh)
                seg = jnp.exp(jnp.where(causal, csb - cst[hh:hh + 1, :], neg_inf))
                lhs += [(cbm * seg * dtt[hh:hh + 1, :]).astype(BF16), (cg * jnp.exp(csb)).astype(BF16)]
                rhs += [xm, jnp.where(jmask, hb, jnp.zeros_like(hb))]
                st_lhs.append((bgt * toendt[hh:hh + 1, :]).astype(BF16))
                st_rhs.append(xm)
                decay_row = jnp.where(jmask, dec_last[:, hh:hh + 1], decay_row)
            yg = dsk_ref[:, gcols] * xg + jnp.dot(jnp.concatenate(lhs, axis=1), jnp.concatenate(rhs, axis=0),
                                                  preferred_element_type=F32)
            st_ref[g] = st_prev * decay_row + jnp.dot(jnp.concatenate(st_lhs, axis=1),
                                                      jnp.concatenate(st_rhs, axis=0), preferred_element_type=F32)
            yz = yg * proj_ref[rows, S_Z + g * SSD_GW:S_Z + (g + 1) * SSD_GW]
            ycat_ref[rows, gcols] = _rmsnorm(yz, nw_ref[:, gcols], 1e-5).astype(BF16)
        return carry

    lax.fori_loop(0, n_chunks, chunk_body, 0)
    o_ref[...] = h_ref[...] + jnp.dot(ycat_ref[...], wout_ref[...], preferred_element_type=F32)


def _ssd_core(proj, h, dtb, alog, dsk, nw_row, wout):
    b, s, d = h.shape
    ts = TS_CORE
    return pl.pallas_call(
        _ssd_core_kernel,
        out_shape=jax.ShapeDtypeStruct((b, s, d), F32),
        grid=(b, s // ts),
        in_specs=[
            pl.BlockSpec((None, ts, S_PROJ), lambda bi, i: (bi, i, 0)),
            pl.BlockSpec((None, ts, d), lambda bi, i: (bi, i, 0)),
            _resident((1, LANES)),
            _resident((1, LANES)),
            _resident((1, SSD_INNER)),
            _resident((1, SSD_INNER)),
            _resident((SSD_INNER, d)),
        ],
        out_specs=pl.BlockSpec((None, ts, d), lambda bi, i: (bi, i, 0)),
        scratch_shapes=[
            pltpu.VMEM((SSD_GROUPS, SSD_N, SSD_GW), F32),
            pltpu.VMEM((ts, SSD_INNER), BF16),
        ],
        compiler_params=_compiler_params(("parallel", "arbitrary")),
        name="ssd_core",
    )(proj, h, dtb, alog, dsk, nw_row, wout)


def _pad_lanes(row, width=LANES):
    return jnp.pad(row, ((0, 0), (0, width - row.shape[1])))


def _hybrid_layer(h, nw, w_in, i_bias, f_bias, m_norm_w, r_norm_w, w_out, cos_t, sin_t, dec, wq_rep, rrow):
    b, s, d = h.shape
    mq, mk, mv, mi, mf, mo, rq, rk, rv, rg = jnp.split(
        w_in, (256, 512, 1024, 1028, 1032, 1544, 1800, 2056, 2568), axis=1)
    perm = jnp.concatenate([
        (jnp.arange(LANES) // 32) * 64 + jnp.arange(LANES) % 32,
        (jnp.arange(LANES) // 32) * 64 + 32 + jnp.arange(LANES) % 32])
    gate_cols = _pad_lanes
    w_cat = jnp.concatenate([mq, mk, mv, mo, rq[:, perm], rk[:, perm], rv, rg, gate_cols(mi), gate_cols(mf)],
                            axis=1).astype(BF16)
    gbias = jnp.pad(jnp.concatenate([gate_cols(i_bias[None, :]), gate_cols(f_bias[None, :])], axis=0),
                    ((0, SUBLANES - 2), (0, 0)))
    proj = _norm_matmul(h.reshape(b * s, d), nw[None, :], w_cat).reshape(b, s, H_PROJ)
    nw_row = jnp.concatenate([m_norm_w, r_norm_w])[None, :]
    return _hyb_core(proj, h, cos_t, sin_t, gbias, nw_row, dec, wq_rep, rrow, w_out.astype(BF16))


def _ssd_layer(h, nw, w_in, conv_w, conv_b, dt_bias, a_log, d_skip, norm_w, w_out):
    b, s, d = h.shape
    w_cat = jnp.pad(w_in, ((0, 0), (0, S_PROJ - w_in.shape[1]))).astype(BF16)
    proj = _ssd_proj(h, nw[None, :], w_cat, conv_w, conv_b[None, :])
    dsk = jnp.repeat(d_skip, SSD_P)[None, :]
    return _ssd_core(proj, h, _pad_lanes(dt_bias[None, :]), _pad_lanes(a_log[None, :]), dsk, norm_w[None, :],
                     w_out.astype(BF16))


def _position_tables(seq):
    dk = 64
    inv = ROPE_BASE ** (-jnp.arange(0, dk, 2, dtype=F32) / dk)
    ang = jnp.arange(seq, dtype=F32)[:, None] * inv[None, :]
    cos_t = jnp.tile(jnp.cos(ang), (1, RET_HEADS))
    sin_t = jnp.tile(jnp.sin(ang), (1, RET_HEADS))
    log_gamma = jnp.log(1.0 - 2.0 ** (-5.0 - jnp.arange(RET_HEADS, dtype=F32)))
    idx = jnp.arange(CHUNK, dtype=F32)
    rel = idx[:, None] - idx[None, :]
    dec = jnp.where((rel >= 0)[None], jnp.exp(jnp.maximum(rel, 0.0)[None] * log_gamma[:, None, None]), 0.0)
    w_q = jnp.exp((idx + 1.0)[:, None] * log_gamma)
    w_k = jnp.exp((CHUNK - 1.0 - idx)[:, None] * log_gamma)
    chunk_decay = jnp.broadcast_to(jnp.exp(CHUNK * log_gamma)[:, None], (RET_HEADS, CHUNK))
    tok = _token_of_row(jnp.arange(CHUNK))
    wq_rep = jnp.broadcast_to(w_q[tok].T[:, :, None], (RET_HEADS, CHUNK, HEAD_V))
    rrow = jnp.concatenate([w_k[tok].T, chunk_decay], axis=0)
    return _interleave_tokens(cos_t, 0), _interleave_tokens(sin_t, 0), dec[:, tok][:, :, tok], wq_rep, rrow


def _interleave_tokens(a, axis):
    shape = a.shape
    split = shape[:axis] + (shape[axis] // CHUNK, SUBLANES, CHUNK // SUBLANES) + shape[axis + 1:]
    return jnp.swapaxes(a.reshape(split), axis + 1, axis + 2).reshape(shape)


def _deinterleave_tokens(a, axis):
    shape = a.shape
    split = shape[:axis] + (shape[axis] // CHUNK, CHUNK // SUBLANES, SUBLANES) + shape[axis + 1:]
    return jnp.swapaxes(a.reshape(split), axis + 1, axis + 2).reshape(shape)


@jax.jit
def kernel(x, norm_mix_w, norm_mlp_w, hyb_w_in, mlstm_i_bias, mlstm_f_bias, mlstm_norm_w, ret_norm_w, hyb_w_out,
           ssd_w_in, ssd_conv_w, ssd_conv_b, ssd_dt_bias, ssd_a_log, ssd_d, ssd_norm_w, ssd_w_out, mlp_w1, mlp_w2,
           final_norm_w):
    b, s, d = x.shape
    depth = norm_mix_w.shape[0]
    cos_t, sin_t, dec, wq_rep, rrow = _position_tables(s)
    h = _interleave_tokens(x, 1)
    for layer in range(depth):
        j = layer // 2
        if layer % 2 == 0:
            h = _hybrid_layer(h, norm_mix_w[layer], hyb_w_in[j], mlstm_i_bias[j], mlstm_f_bias[j], mlstm_norm_w[j],
                              ret_norm_w[j], hyb_w_out[j], cos_t, sin_t, dec, wq_rep, rrow)
        else:
            h = _ssd_layer(h, norm_mix_w[layer], ssd_w_in[j], ssd_conv_w[j], ssd_conv_b[j], ssd_dt_bias[j],
                           ssd_a_log[j], ssd_d[j], ssd_norm_w[j], ssd_w_out[j])
        h = _mlp(h.reshape(b * s, d), norm_mlp_w[layer][None, :], mlp_w1[layer].astype(BF16),
                 mlp_w2[layer].astype(BF16), final_norm_w[None, :], layer == depth - 1).reshape(b, s, d)
    return _deinterleave_tokens(h, 1)
```

```python
import functools

import jax
import jax.numpy as jnp
from jax import lax
from jax.experimental import pallas as pl
from jax.experimental.pallas import tpu as pltpu

F32 = jnp.float32
BF16 = jnp.bfloat16

CHUNK = 128
LANES = 128
SUBLANES = 8
VMEM_LIMIT_BYTES = 56 * 1024 * 1024
ROPE_BASE = 10000.0

MLSTM_HEADS = 4
RET_HEADS = 4
HEAD_QK = 64
QK_ALL = 256
HEAD_V = 128
SSD_GROUPS = 8
SSD_REP = 4
SSD_P = 64
SSD_N = 128
SSD_HEADS = SSD_GROUPS * SSD_REP
SSD_GW = SSD_REP * SSD_P
SSD_INNER = SSD_GROUPS * SSD_GW
SSD_CONV = 4

H_MQ, H_MK, H_MV, H_MO = 0, 256, 512, 1024
H_RQ, H_RK, H_RV, H_RG = 1536, 1792, 2048, 2560
H_GI, H_GF = 3072, 3200
H_PROJ = H_GF + LANES
S_Z, S_X, S_B, S_C, S_DT = 0, 2048, 4096, 5120, 6144
S_PROJ = S_DT + LANES
S_CONV_DIM = S_DT - S_X

TM_PROJ = 512
TM_SSD_PROJ = 256
CONV_COLS = 512
HALO_SLABS = SSD_CONV - 1
TM_MLP = 512
TS_CORE = 512
FF_CHUNK = 1024


def _log1p_exp_neg_abs(x):
    return jnp.log1p(jnp.exp(-jnp.abs(x)))


def _softplus(x):
    return jnp.maximum(x, 0.0) + _log1p_exp_neg_abs(x)


def _log_sigmoid(x):
    return jnp.minimum(x, 0.0) - _log1p_exp_neg_abs(x)


def _sigmoid(x):
    return 0.5 * jnp.tanh(0.5 * x) + 0.5


def _silu(x):
    half = 0.5 * x
    return half * jnp.tanh(half) + half


def _rmsnorm(x, w_row, eps):
    return x * lax.rsqrt(jnp.mean(x * x, axis=-1, keepdims=True) + eps) * w_row


def _token_of_row(r):
    return (r % SUBLANES) * (CHUNK // SUBLANES) + r // SUBLANES


def _causal_mask():
    row = lax.broadcasted_iota(jnp.int32, (CHUNK, CHUNK), 0)
    col = lax.broadcasted_iota(jnp.int32, (CHUNK, CHUNK), 1)
    return _token_of_row(row) >= _token_of_row(col)


def _split3(x):
    hi = x.astype(BF16)
    r1 = x - hi.astype(F32)
    mid = r1.astype(BF16)
    lo = (r1 - mid.astype(F32)).astype(BF16)
    return hi, mid, lo


def _chunk_cumsum(tri3, x):
    return jnp.dot(tri3, jnp.concatenate(_split3(x), axis=0), preferred_element_type=F32)


def _lane_replicate(x, c):
    return jnp.broadcast_to(x[:, c:c + 1], x.shape)


def _compiler_params(semantics):
    return pltpu.CompilerParams(dimension_semantics=semantics, vmem_limit_bytes=VMEM_LIMIT_BYTES)


def _resident(shape):
    zeros = (0,) * len(shape)
    return pl.BlockSpec(shape, lambda *_: zeros, pipeline_mode=pl.Buffered(1))


def _norm_matmul_kernel(x_ref, nw_ref, w_ref, o_ref):
    xn = _rmsnorm(x_ref[...], nw_ref[...], 1e-6).astype(BF16)
    o_ref[...] = jnp.dot(xn, w_ref[...], preferred_element_type=F32)


def _norm_matmul(x2d, nw_row, w_bf16):
    t, d = x2d.shape
    n = w_bf16.shape[1]
    return pl.pallas_call(
        _norm_matmul_kernel,
        out_shape=jax.ShapeDtypeStruct((t, n), F32),
        grid=(t // TM_PROJ,),
        in_specs=[pl.BlockSpec((TM_PROJ, d), lambda i: (i, 0)), _resident((1, d)), _resident((d, n))],
        out_specs=pl.BlockSpec((TM_PROJ, n), lambda i: (i, 0)),
        compiler_params=_compiler_params(("parallel",)),
        name="norm_matmul",
    )(x2d, nw_row, w_bf16)


def _causal_conv_silu(raw, prev_tail, cw_ref, cb_ref, cc, last_sublane):
    wrapped = []
    for k in range(HALO_SLABS):
        cur = raw[CHUNK - (HALO_SLABS - k) * SUBLANES:CHUNK - (HALO_SLABS - k - 1) * SUBLANES]
        prv = prev_tail[k * SUBLANES:(k + 1) * SUBLANES]
        wrapped.append(pltpu.roll(jnp.where(last_sublane, prv, cur), 1, 0))
    acc = raw * cw_ref[SSD_CONV - 1:SSD_CONV, cc] + cb_ref[:, cc]
    for j in range(1, SSD_CONV):
        shifted = jnp.concatenate(wrapped[HALO_SLABS - j:] + [raw[:CHUNK - j * SUBLANES]], axis=0)
        acc = acc + shifted * cw_ref[SSD_CONV - 1 - j:SSD_CONV - j, cc]
    return _silu(acc)


def _ssd_proj_kernel(x_ref, nw_ref, w_ref, cw_ref, cb_ref, o_ref, halo_ref):
    @pl.when(pl.program_id(1) == 0)
    def _init():
        halo_ref[...] = jnp.zeros_like(halo_ref)

    xn = _rmsnorm(x_ref[...], nw_ref[...], 1e-6).astype(BF16)
    last_sublane = lax.broadcasted_iota(jnp.int32, (SUBLANES, CONV_COLS), 0) == SUBLANES - 1
    n_chunks = x_ref.shape[0] // CHUNK
    tail = slice(CHUNK - HALO_SLABS * SUBLANES, CHUNK)

    for cb in range(S_X // CONV_COLS):
        cols = slice(cb * CONV_COLS, (cb + 1) * CONV_COLS)
        o_ref[:, cols] = _silu(jnp.dot(xn, w_ref[:, cols], preferred_element_type=F32))
    o_ref[:, S_DT:] = jnp.dot(xn, w_ref[:, S_DT:], preferred_element_type=F32)
    for cb in range(S_CONV_DIM // CONV_COLS):
        cc = slice(cb * CONV_COLS, (cb + 1) * CONV_COLS)
        pc = slice(S_X + cb * CONV_COLS, S_X + (cb + 1) * CONV_COLS)
        raw_all = jnp.dot(xn, w_ref[:, pc], preferred_element_type=F32)
        prev_tail = halo_ref[:, cc]
        for c in range(n_chunks):
            raw = raw_all[c * CHUNK:(c + 1) * CHUNK]
            o_ref[c * CHUNK:(c + 1) * CHUNK, pc] = _causal_conv_silu(raw, prev_tail, cw_ref, cb_ref, cc, last_sublane)
            prev_tail = raw[tail]
        halo_ref[:, cc] = prev_tail


def _ssd_proj(h, nw_row, w_bf16, cw, cb_row):
    b, s, d = h.shape
    tm = TM_SSD_PROJ
    return pl.pallas_call(
        _ssd_proj_kernel,
        out_shape=jax.ShapeDtypeStruct((b, s, S_PROJ), F32),
        grid=(b, s // tm),
        in_specs=[pl.BlockSpec((None, tm, d), lambda bi, i: (bi, i, 0)), _resident((1, d)), _resident((d, S_PROJ)),
                  _resident((SSD_CONV, S_CONV_DIM)), _resident((1, S_CONV_DIM))],
        out_specs=pl.BlockSpec((None, tm, S_PROJ), lambda bi, i: (bi, i, 0)),
        scratch_shapes=[pltpu.VMEM((HALO_SLABS * SUBLANES, S_CONV_DIM), F32)],
        compiler_params=_compiler_params(("parallel", "arbitrary")),
        name="ssd_proj",
    )(h, nw_row, w_bf16, cw, cb_row)


def _mlp_kernel(x_ref, nw_ref, w1_ref, w2_ref, fnw_ref, o_ref, *, final):
    x = x_ref[...]
    xn = _rmsnorm(x, nw_ref[...], 1e-6).astype(BF16)
    acc = x
    for f in range(w1_ref.shape[1] // FF_CHUNK):
        cols = slice(f * FF_CHUNK, (f + 1) * FF_CHUNK)
        hid = jnp.maximum(jnp.dot(xn, w1_ref[:, cols], preferred_element_type=F32), 0.0)
        acc = acc + jnp.dot((hid * hid).astype(BF16), w2_ref[cols, :], preferred_element_type=F32)
    if final:
        acc = _rmsnorm(acc, fnw_ref[...], 1e-6)
    o_ref[...] = acc


def _mlp(x2d, nw_row, w1, w2, fnw_row, final):
    t, d = x2d.shape
    dff = w1.shape[1]
    return pl.pallas_call(
        functools.partial(_mlp_kernel, final=final),
        out_shape=jax.ShapeDtypeStruct((t, d), F32),
        grid=(t // TM_MLP,),
        in_specs=[pl.BlockSpec((TM_MLP, d), lambda i: (i, 0)), _resident((1, d)), _resident((d, dff)),
                  _resident((dff, d)), _resident((1, d))],
        out_specs=pl.BlockSpec((TM_MLP, d), lambda i: (i, 0)),
        compiler_params=_compiler_params(("parallel",)),
        name="mlp",
    )(x2d, nw_row, w1, w2, fnw_row)


def _shift_tokens(x, d, fill):
    n_slab = CHUNK // SUBLANES
    sub = lax.broadcasted_iota(jnp.int32, (SUBLANES, x.shape[1]), 0)
    slabs = [x[i * SUBLANES:(i + 1) * SUBLANES] for i in range(n_slab)]
    if d < n_slab:
        wrapped = [jnp.where(sub == 0, fill, pltpu.roll(slabs[n_slab - d + i], 1, 0)) for i in range(d)]
        return jnp.concatenate(wrapped + slabs[:n_slab - d], axis=0)
    k = d // n_slab
    return jnp.concatenate([jnp.where(sub < k, fill, pltpu.roll(sl, k, 0)) for sl in slabs], axis=0)


def _token_cummax(x):
    d = 1
    while d < CHUNK:
        x = jnp.maximum(x, _shift_tokens(x, d, -jnp.inf))
        d *= 2
    return x


def _embed_rows(block, starts):
    piece = block.shape[0] // len(starts)
    parts, pos = [], 0
    for i, start in enumerate(starts):
        if start > pos:
            parts.append(jnp.zeros((start - pos, block.shape[1]), block.dtype))
        parts.append(block[i * piece:(i + 1) * piece])
        pos = start + piece
    if pos < QK_ALL:
        parts.append(jnp.zeros((QK_ALL - pos, block.shape[1]), block.dtype))
    return jnp.concatenate(parts, axis=0)


def _hyb_core_kernel(proj_ref, h_ref, cos_ref, sin_ref, gbias_ref, nw_ref, dec_ref, wq_ref, rrow_ref,
                     wout_ref, o_ref, cst_ref, mst_ref, rst_ref, x_ref, dn_ref, ycat_ref):
    @pl.when(pl.program_id(1) == 0)
    def _init():
        cst_ref[...] = jnp.zeros_like(cst_ref)
        mst_ref[...] = jnp.zeros_like(mst_ref)
        rst_ref[...] = jnp.zeros_like(rst_ref)

    causal = _causal_mask()
    tri3 = jnp.tile(causal.astype(BF16), (1, 3))
    lane_qk = lax.broadcasted_iota(jnp.int32, (1, QK_ALL), 1)
    ones_v = jnp.ones((CHUNK, HEAD_V), BF16)
    neg_inf = jnp.float32(-jnp.inf)
    n_chunks = proj_ref.shape[0] // CHUNK
    n_heads = MLSTM_HEADS + RET_HEADS
    half = HEAD_QK // 2

    def finish_chunk(rows):
        x = x_ref[...]
        d = x - jnp.mean(x, axis=-1, keepdims=True)
        var = jnp.mean(d * d, axis=-1, keepdims=True)
        dn = dn_ref[...]
        eps = jnp.concatenate([1e-5 * dn * dn, jnp.full((RET_HEADS * CHUNK, HEAD_V), 1e-5, F32)], axis=0)
        y = d * lax.rsqrt(var + eps)
        for h in range(n_heads):
            cols = slice(h * HEAD_V, (h + 1) * HEAD_V)
            if h < MLSTM_HEADS:
                gate = _sigmoid(proj_ref[rows, H_MO + h * HEAD_V:H_MO + (h + 1) * HEAD_V])
            else:
                hr = h - MLSTM_HEADS
                gate = _silu(proj_ref[rows, H_RG + hr * HEAD_V:H_RG + (hr + 1) * HEAD_V])
            ycat_ref[rows, cols] = (gate * (y[h * CHUNK:(h + 1) * CHUNK] * nw_ref[:, cols])).astype(BF16)

    def chunk_body(c, carry):
        rows = pl.ds(pl.multiple_of(c * CHUNK, CHUNK), CHUNK)

        gi = proj_ref[rows, H_GI:H_GI + LANES] + gbias_ref[0:1, :]
        gf = proj_ref[rows, H_GF:H_GF + LANES] + gbias_ref[1:2, :]
        bc = _chunk_cumsum(tri3, _log_sigmoid(gf))
        r = gi - bc
        m_prev = mst_ref[0:1, :]
        mu = jnp.maximum(m_prev, _token_cummax(r))
        g_row = bc[CHUNK - 1:CHUNK, :]
        m_new = jnp.maximum(g_row + m_prev, jnp.max(g_row + r, axis=0, keepdims=True))
        s_old = jnp.exp(g_row + m_prev - m_new)
        wt = jnp.exp(g_row + r - m_new).T
        rt = (r - m_prev).T
        mst_ref[0:1, :] = m_new
        alpha = mu - m_prev
        beta = bc + mu

        mq = (proj_ref[rows, H_MQ:H_MQ + QK_ALL] * 0.125).astype(BF16)
        mkt_f = proj_ref[rows, H_MK:H_MK + QK_ALL].T
        qhs = [jnp.where(lane_qk // HEAD_QK == h, mq, jnp.zeros_like(mq)) for h in range(MLSTM_HEADS)]
        scores = jnp.dot(jnp.concatenate(qhs, axis=0), mkt_f.astype(BF16), preferred_element_type=F32)
        for h in range(MLSTM_HEADS):
            hrows = slice(h * CHUNK, (h + 1) * CHUNK)
            alpha_b = _lane_replicate(alpha, h)
            beta_b = _lane_replicate(beta, h)
            c_prev = cst_ref[h]
            vext = jnp.concatenate(
                [proj_ref[rows, H_MV + h * HEAD_V:H_MV + (h + 1) * HEAD_V].astype(BF16), ones_v], axis=1)

            dmat = jnp.exp(jnp.where(causal, rt[h:h + 1, :] - alpha_b, neg_inf))
            s = scores[hrows] * dmat
            kw = (mkt_f[h * HEAD_QK:(h + 1) * HEAD_QK] * wt[h:h + 1, :]).astype(BF16)
            both = jnp.dot(jnp.concatenate([s.astype(BF16), kw], axis=0), vext, preferred_element_type=F32)
            inter = jnp.dot(qhs[h], _embed_rows(c_prev.astype(BF16), [h * HEAD_QK]),
                            preferred_element_type=F32)
            si_b = jnp.exp(-alpha_b)
            num_ext = both[:CHUNK] + jnp.concatenate([si_b, si_b], axis=1) * inter
            x_ref[hrows, :] = num_ext[:, :HEAD_V]
            dn_ref[hrows, :] = jnp.maximum(jnp.abs(num_ext[:, HEAD_V:]), jnp.exp(-beta_b))
            cst_ref[h] = s_old[:, h:h + 1] * c_prev + both[CHUNK:]

        cs, sn = cos_ref[rows, :], sin_ref[rows, :]
        q1, q2 = proj_ref[rows, H_RQ:H_RQ + LANES], proj_ref[rows, H_RQ + LANES:H_RQ + 2 * LANES]
        k1, k2 = proj_ref[rows, H_RK:H_RK + LANES], proj_ref[rows, H_RK + LANES:H_RK + 2 * LANES]
        rq = (jnp.concatenate([q1 * cs - q2 * sn, q1 * sn + q2 * cs], axis=1) * 0.125).astype(BF16)
        rkt_f = jnp.concatenate([k1 * cs - k2 * sn, k1 * sn + k2 * cs], axis=1).T
        qhs = [jnp.where((lane_qk % LANES) // half == h, rq, jnp.zeros_like(rq)) for h in range(RET_HEADS)]
        scores = jnp.dot(jnp.concatenate(qhs, axis=0), rkt_f.astype(BF16), preferred_element_type=F32)
        for h in range(RET_HEADS):
            hrows = slice((MLSTM_HEADS + h) * CHUNK, (MLSTM_HEADS + h + 1) * CHUNK)
            v = proj_ref[rows, H_RV + h * HEAD_V:H_RV + (h + 1) * HEAD_V].astype(BF16)
            r_prev = rst_ref[h]
            s = scores[h * CHUNK:(h + 1) * CHUNK] * dec_ref[h]
            k_own = jnp.concatenate([rkt_f[h * half:(h + 1) * half], rkt_f[LANES + h * half:LANES + (h + 1) * half]],
                                    axis=0)
            kw = (k_own * rrow_ref[h:h + 1, :]).astype(BF16)
            both = jnp.dot(jnp.concatenate([s.astype(BF16), kw], axis=0), v, preferred_element_type=F32)
            inter = jnp.dot(qhs[h], _embed_rows(r_prev.astype(BF16), [h * half, LANES + h * half]),
                            preferred_element_type=F32)
            x_ref[hrows, :] = both[:CHUNK] + wq_ref[h] * inter
            rst_ref[h] = rrow_ref[RET_HEADS + h:RET_HEADS + h + 1, 0:1] * r_prev + both[CHUNK:]

        finish_chunk(rows)
        return carry

    lax.fori_loop(0, n_chunks, chunk_body, 0)
    o_ref[...] = h_ref[...] + jnp.dot(ycat_ref[...], wout_ref[...], preferred_element_type=F32)


def _hyb_core(proj, h, cos_t, sin_t, gbias, nw_row, dec, wq_rep, rrow, wout):
    b, s, d = h.shape
    ts = TS_CORE
    inner = wout.shape[0]
    n_heads = MLSTM_HEADS + RET_HEADS
    return pl.pallas_call(
        _hyb_core_kernel,
        out_shape=jax.ShapeDtypeStruct((b, s, d), F32),
        grid=(b, s // ts),
        in_specs=[
            pl.BlockSpec((None, ts, H_PROJ), lambda bi, i: (bi, i, 0)),
            pl.BlockSpec((None, ts, d), lambda bi, i: (bi, i, 0)),
            pl.BlockSpec((ts, LANES), lambda bi, i: (i, 0)),
            pl.BlockSpec((ts, LANES), lambda bi, i: (i, 0)),
            _resident((SUBLANES, LANES)),
            _resident((1, inner)),
            _resident((RET_HEADS, CHUNK, CHUNK)),
            _resident((RET_HEADS, CHUNK, HEAD_V)),
            _resident((SUBLANES, LANES)),
            _resident((inner, d)),
        ],
        out_specs=pl.BlockSpec((None, ts, d), lambda bi, i: (bi, i, 0)),
        scratch_shapes=[
            pltpu.VMEM((MLSTM_HEADS, HEAD_QK, 2 * HEAD_V), F32),
            pltpu.VMEM((SUBLANES, LANES), F32),
            pltpu.VMEM((RET_HEADS, HEAD_QK, HEAD_V), F32),
            pltpu.VMEM((n_heads * CHUNK, HEAD_V), F32),
            pltpu.VMEM((MLSTM_HEADS * CHUNK, HEAD_V), F32),
            pltpu.VMEM((ts, inner), BF16),
        ],
        compiler_params=_compiler_params(("parallel", "arbitrary")),
        name="hybrid_core",
    )(proj, h, cos_t, sin_t, gbias, nw_row, dec, wq_rep, rrow, wout)


def _ssd_core_kernel(proj_ref, h_ref, dtb_ref, alog_ref, dsk_ref, nw_ref, wout_ref, o_ref, st_ref, ycat_ref):
    @pl.when(pl.program_id(1) == 0)
    def _init():
        st_ref[...] = jnp.zeros_like(st_ref)

    causal = _causal_mask()
    tri3 = jnp.tile(causal.astype(BF16), (1, 3))
    lane_g = lax.broadcasted_iota(jnp.int32, (1, SSD_GW), 1)
    neg_inf = jnp.float32(-jnp.inf)
    a_row = -jnp.exp(alog_ref[...])
    n_chunks = proj_ref.shape[0] // CHUNK

    def chunk_body(c, carry):
        rows = pl.ds(pl.multiple_of(c * CHUNK, CHUNK), CHUNK)

        dtp = _softplus(proj_ref[rows, S_DT:S_DT + LANES] + dtb_ref[...])
        cs = _chunk_cumsum(tri3, dtp * a_row)
        cst = cs.T
        dtt = dtp.T
        toendt = jnp.exp(cst[:, CHUNK - 1:CHUNK] - cst) * dtt
        dec_last = jnp.exp(cs[CHUNK - 1:CHUNK, :])

        for g in range(SSD_GROUPS):
            gcols = slice(g * SSD_GW, (g + 1) * SSD_GW)
            bg = proj_ref[rows, S_B + g * SSD_N:S_B + (g + 1) * SSD_N]
            cg = proj_ref[rows, S_C + g * SSD_N:S_C + (g + 1) * SSD_N]
            xg = proj_ref[rows, S_X + g * SSD_GW:S_X + (g + 1) * SSD_GW]
            cbm = lax.dot_general(cg.astype(BF16), bg.astype(BF16), (((1,), (1,)), ((), ())),
                                  preferred_element_type=F32)
            bgt = bg.T
            xb = xg.astype(BF16)
            st_prev = st_ref[g]
            hb = st_prev.astype(BF16)
            decay_row = jnp.zeros((1, SSD_GW), F32)
            lhs, rhs, st_lhs, st_rhs = [], [], [], []
            for j in range(SSD_REP):
                hh = g * SSD_REP + j
                jmask = lane_g // SSD_P == j
                xm = jnp.where(jmask, xb, jnp.zeros_like(xb))
                csb = _lane_replicate(cs, hh)
                seg = jnp.exp(jnp.where(causal, csb - cst[hh:hh + 1, :], neg_inf))
                lhs += [(cbm * seg * dtt[hh:hh + 1, :]).astype(BF16), (cg * jnp.exp(csb)).astype(BF16)]
                rhs += [xm, jnp.where(jmask, hb, jnp.zeros_like(hb))]
                st_lhs.append((bgt * toendt[hh:hh + 1, :]).astype(BF16))
                st_rhs.append(xm)
                decay_row = jnp.where(jmask, dec_last[:, hh:hh + 1], decay_row)
            yg = dsk_ref[:, gcols] * xg + jnp.dot(jnp.concatenate(lhs, axis=1), jnp.concatenate(rhs, axis=0),
                                                  preferred_element_type=F32)
            st_ref[g] = st_prev * decay_row + jnp.dot(jnp.concatenate(st_lhs, axis=1),
                                                      jnp.concatenate(st_rhs, axis=0), preferred_element_type=F32)
            yz = yg * proj_ref[rows, S_Z + g * SSD_GW:S_Z + (g + 1) * SSD_GW]
            ycat_ref[rows, gcols] = _rmsnorm(yz, nw_ref[:, gcols], 1e-5).astype(BF16)
        return carry

    lax.fori_loop(0, n_chunks, chunk_body, 0)
    o_ref[...] = h_ref[...] + jnp.dot(ycat_ref[...], wout_ref[...], preferred_element_type=F32)


def _ssd_core(proj, h, dtb, alog, dsk, nw_row, wout):
    b, s, d = h.shape
    ts = TS_CORE
    return pl.pallas_call(
        _ssd_core_kernel,
        out_shape=jax.ShapeDtypeStruct((b, s, d), F32),
        grid=(b, s // ts),
        in_specs=[
            pl.BlockSpec((None, ts, S_PROJ), lambda bi, i: (bi, i, 0)),
            pl.BlockSpec((None, ts, d), lambda bi, i: (bi, i, 0)),
            _resident((1, LANES)),
            _resident((1, LANES)),
            _resident((1, SSD_INNER)),
            _resident((1, SSD_INNER)),
            _resident((SSD_INNER, d)),
        ],
        out_specs=pl.BlockSpec((None, ts, d), lambda bi, i: (bi, i, 0)),
        scratch_shapes=[
            pltpu.VMEM((SSD_GROUPS, SSD_N, SSD_GW), F32),
            pltpu.VMEM((ts, SSD_INNER), BF16),
        ],
        compiler_params=_compiler_params(("parallel", "arbitrary")),
        name="ssd_core",
    )(proj, h, dtb, alog, dsk, nw_row, wout)


def _pad_lanes(row, width=LANES):
    return jnp.pad(row, ((0, 0), (0, width - row.shape[1])))


def _hybrid_layer(h, nw, w_in, i_bias, f_bias, m_norm_w, r_norm_w, w_out, cos_t, sin_t, dec, wq_rep, rrow):
    b, s, d = h.shape
    mq, mk, mv, mi, mf, mo, rq, rk, rv, rg = jnp.split(
        w_in, (256, 512, 1024, 1028, 1032, 1544, 1800, 2056, 2568), axis=1)
    perm = jnp.concatenate([
        (jnp.arange(LANES) // 32) * 64 + jnp.arange(LANES) % 32,
        (jnp.arange(LANES) // 32) * 64 + 32 + jnp.arange(LANES) % 32])
    gate_cols = _pad_lanes
    w_cat = jnp.concatenate([mq, mk, mv, mo, rq[:, perm], rk[:, perm], rv, rg, gate_cols(mi), gate_cols(mf)],
                            axis=1).astype(BF16)
    gbias = jnp.pad(jnp.concatenate([gate_cols(i_bias[None, :]), gate_cols(f_bias[None, :])], axis=0),
                    ((0, SUBLANES - 2), (0, 0)))
    proj = _norm_matmul(h.reshape(b * s, d), nw[None, :], w_cat).reshape(b, s, H_PROJ)
    nw_row = jnp.concatenate([m_norm_w, r_norm_w])[None, :]
    return _hyb_core(proj, h, cos_t, sin_t, gbias, nw_row, dec, wq_rep, rrow, w_out.astype(BF16))


def _ssd_layer(h, nw, w_in, conv_w, conv_b, dt_bias, a_log, d_skip, norm_w, w_out):
    b, s, d = h.shape
    w_cat = jnp.pad(w_in, ((0, 0), (0, S_PROJ - w_in.shape[1]))).astype(BF16)
    proj = _ssd_proj(h, nw[None, :], w_cat, conv_w, conv_b[None, :])
    dsk = jnp.repeat(d_skip, SSD_P)[None, :]
    return _ssd_core(proj, h, _pad_lanes(dt_bias[None, :]), _pad_lanes(a_log[None, :]), dsk, norm_w[None, :],
                     w_out.astype(BF16))


def _position_tables(seq):
    dk = 64
    inv = ROPE_BASE ** (-jnp.arange(0, dk, 2, dtype=F32) / dk)
    ang = jnp.arange(seq, dtype=F32)[:, None] * inv[None, :]
    cos_t = jnp.tile(jnp.cos(ang), (1, RET_HEADS))
    sin_t = jnp.tile(jnp.sin(ang), (1, RET_HEADS))
    log_gamma = jnp.log(1.0 - 2.0 ** (-5.0 - jnp.arange(RET_HEADS, dtype=F32)))
    idx = jnp.arange(CHUNK, dtype=F32)
    rel = idx[:, None] - idx[None, :]
    dec = jnp.where((rel >= 0)[None], jnp.exp(jnp.maximum(rel, 0.0)[None] * log_gamma[:, None, None]), 0.0)
    w_q = jnp.exp((idx + 1.0)[:, None] * log_gamma)
    w_k = jnp.exp((CHUNK - 1.0 - idx)[:, None] * log_gamma)
    chunk_decay = jnp.broadcast_to(jnp.exp(CHUNK * log_gamma)[:, None], (RET_HEADS, CHUNK))
    tok = _token_of_row(jnp.arange(CHUNK))
    wq_rep = jnp.broadcast_to(w_q[tok].T[:, :, None], (RET_HEADS, CHUNK, HEAD_V))
    rrow = jnp.concatenate([w_k[tok].T, chunk_decay], axis=0)
    return _interleave_tokens(cos_t, 0), _interleave_tokens(sin_t, 0), dec[:, tok][:, :, tok], wq_rep, rrow


def _interleave_tokens(a, axis):
    shape = a.shape
    split = shape[:axis] + (shape[axis] // CHUNK, SUBLANES, CHUNK // SUBLANES) + shape[axis + 1:]
    return jnp.swapaxes(a.reshape(split), axis + 1, axis + 2).reshape(shape)


def _deinterleave_tokens(a, axis):
    shape = a.shape
    split = shape[:axis] + (shape[axis] // CHUNK, CHUNK // SUBLANES, SUBLANES) + shape[axis + 1:]
    return jnp.swapaxes(a.reshape(split), axis + 1, axis + 2).reshape(shape)


@jax.jit
def kernel(x, norm_mix_w, norm_mlp_w, hyb_w_in, mlstm_i_bias, mlstm_f_bias, mlstm_norm_w, ret_norm_w, hyb_w_out,
           ssd_w_in, ssd_conv_w, ssd_conv_b, ssd_dt_bias, ssd_a_log, ssd_d, ssd_norm_w, ssd_w_out, mlp_w1, mlp_w2,
           final_norm_w):
    b, s, d = x.shape
    depth = norm_mix_w.shape[0]
    cos_t, sin_t, dec, wq_rep, rrow = _position_tables(s)
    h = _interleave_tokens(x, 1)
    for layer in range(depth):
        j = layer // 2
        if layer % 2 == 0:
            h = _hybrid_layer(h, norm_mix_w[layer], hyb_w_in[j], mlstm_i_bias[j], mlstm_f_bias[j], mlstm_norm_w[j],
                              ret_norm_w[j], hyb_w_out[j], cos_t, sin_t, dec, wq_rep, rrow)
        else:
            h = _ssd_layer(h, norm_mix_w[layer], ssd_w_in[j], ssd_conv_w[j], ssd_conv_b[j], ssd_dt_bias[j],
                           ssd_a_log[j], ssd_d[j], ssd_norm_w[j], ssd_w_out[j])
        h = _mlp(h.reshape(b * s, d), norm_mlp_w[layer][None, :], mlp_w1[layer].astype(BF16),
                 mlp_w2[layer].astype(BF16), final_norm_w[None, :], layer == depth - 1).reshape(b, s, d)
    return _deinterleave_tokens(h, 1)
```

```python
import functools

import jax
import jax.numpy as jnp
from jax import lax
from jax.experimental import pallas as pl
from jax.experimental.pallas import tpu as pltpu

F32 = jnp.float32
BF16 = jnp.bfloat16

CHUNK = 128
LANES = 128
SUBLANES = 8
VMEM_LIMIT_BYTES = 56 * 1024 * 1024
ROPE_BASE = 10000.0

MLSTM_HEADS = 4
RET_HEADS = 4
HEAD_QK = 64
QK_ALL = 256
HEAD_V = 128
SSD_GROUPS = 8
SSD_REP = 4
SSD_P = 64
SSD_N = 128
SSD_HEADS = SSD_GROUPS * SSD_REP
SSD_GW = SSD_REP * SSD_P
SSD_INNER = SSD_GROUPS * SSD_GW
SSD_CONV = 4

H_MQ, H_MK, H_MV, H_MO = 0, 256, 512, 1024
H_RQ, H_RK, H_RV, H_RG = 1536, 1792, 2048, 2560
H_GI, H_GF = 3072, 3200
H_PROJ = H_GF + LANES
S_Z, S_X, S_B, S_C, S_DT = 0, 2048, 4096, 5120, 6144
S_PROJ = S_DT + LANES
S_CONV_DIM = S_DT - S_X

TM_PROJ = 512
TM_SSD_PROJ = 256
CONV_COLS = 512
HALO_SLABS = SSD_CONV - 1
TM_MLP = 512
TS_CORE = 512
FF_CHUNK = 1024


def _log1p_exp_neg_abs(x):
    return jnp.log1p(jnp.exp(-jnp.abs(x)))


def _softplus(x):
    return jnp.maximum(x, 0.0) + _log1p_exp_neg_abs(x)


def _log_sigmoid(x):
    return jnp.minimum(x, 0.0) - _log1p_exp_neg_abs(x)


def _sigmoid(x):
    return 0.5 * jnp.tanh(0.5 * x) + 0.5


def _silu(x):
    half = 0.5 * x
    return half * jnp.tanh(half) + half


def _rmsnorm(x, w_row, eps):
    return x * lax.rsqrt(jnp.mean(x * x, axis=-1, keepdims=True) + eps) * w_row


def _token_of_row(r):
    return (r % SUBLANES) * (CHUNK // SUBLANES) + r // SUBLANES


def _causal_mask():
    row = lax.broadcasted_iota(jnp.int32, (CHUNK, CHUNK), 0)
    col = lax.broadcasted_iota(jnp.int32, (CHUNK, CHUNK), 1)
    return _token_of_row(row) >= _token_of_row(col)


def _lane_replicate(x, c):
    return jnp.broadcast_to(x[:, c:c + 1], x.shape)


def _compiler_params(semantics):
    return pltpu.CompilerParams(dimension_semantics=semantics, vmem_limit_bytes=VMEM_LIMIT_BYTES)


def _resident(shape):
    zeros = (0,) * len(shape)
    return pl.BlockSpec(shape, lambda *_: zeros, pipeline_mode=pl.Buffered(1))


def _norm_matmul_kernel(x_ref, nw_ref, w_ref, o_ref):
    xn = _rmsnorm(x_ref[...], nw_ref[...], 1e-6).astype(BF16)
    o_ref[...] = jnp.dot(xn, w_ref[...], preferred_element_type=F32)


def _norm_matmul(x2d, nw_row, w_bf16):
    t, d = x2d.shape
    n = w_bf16.shape[1]
    return pl.pallas_call(
        _norm_matmul_kernel,
        out_shape=jax.ShapeDtypeStruct((t, n), F32),
        grid=(t // TM_PROJ,),
        in_specs=[pl.BlockSpec((TM_PROJ, d), lambda i: (i, 0)), _resident((1, d)), _resident((d, n))],
        out_specs=pl.BlockSpec((TM_PROJ, n), lambda i: (i, 0)),
        compiler_params=_compiler_params(("parallel",)),
        name="norm_matmul",
    )(x2d, nw_row, w_bf16)


def _causal_conv_silu(raw, prev_tail, cw_ref, cb_ref, cc, last_sublane):
    wrapped = []
    for k in range(HALO_SLABS):
        cur = raw[CHUNK - (HALO_SLABS - k) * SUBLANES:CHUNK - (HALO_SLABS - k - 1) * SUBLANES]
        prv = prev_tail[k * SUBLANES:(k + 1) * SUBLANES]
        wrapped.append(pltpu.roll(jnp.where(last_sublane, prv, cur), 1, 0))
    acc = raw * cw_ref[SSD_CONV - 1:SSD_CONV, cc] + cb_ref[:, cc]
    for j in range(1, SSD_CONV):
        shifted = jnp.concatenate(wrapped[HALO_SLABS - j:] + [raw[:CHUNK - j * SUBLANES]], axis=0)
        acc = acc + shifted * cw_ref[SSD_CONV - 1 - j:SSD_CONV - j, cc]
    return _silu(acc)


def _ssd_proj_kernel(x_ref, nw_ref, w_ref, cw_ref, cb_ref, o_ref, halo_ref):
    @pl.when(pl.program_id(1) == 0)
    def _init():
        halo_ref[...] = jnp.zeros_like(halo_ref)

    xn = _rmsnorm(x_ref[...], nw_ref[...], 1e-6).astype(BF16)
    last_sublane = lax.broadcasted_iota(jnp.int32, (SUBLANES, CONV_COLS), 0) == SUBLANES - 1
    n_chunks = x_ref.shape[0] // CHUNK
    tail = slice(CHUNK - HALO_SLABS * SUBLANES, CHUNK)

    for cb in range(S_X // CONV_COLS):
        cols = slice(cb * CONV_COLS, (cb + 1) * CONV_COLS)
        o_ref[:, cols] = _silu(jnp.dot(xn, w_ref[:, cols], preferred_element_type=F32))
    o_ref[:, S_DT:] = jnp.dot(xn, w_ref[:, S_DT:], preferred_element_type=F32)
    for cb in range(S_CONV_DIM // CONV_COLS):
        cc = slice(cb * CONV_COLS, (cb + 1) * CONV_COLS)
        pc = slice(S_X + cb * CONV_COLS, S_X + (cb + 1) * CONV_COLS)
        raw_all = jnp.dot(xn, w_ref[:, pc], preferred_element_type=F32)
        prev_tail = halo_ref[:, cc]
        for c in range(n_chunks):
            raw = raw_all[c * CHUNK:(c + 1) * CHUNK]
            o_ref[c * CHUNK:(c + 1) * CHUNK, pc] = _causal_conv_silu(raw, prev_tail, cw_ref, cb_ref, cc, last_sublane)
            prev_tail = raw[tail]
        halo_ref[:, cc] = prev_tail


def _ssd_proj(h, nw_row, w_bf16, cw, cb_row):
    b, s, d = h.shape
    tm = TM_SSD_PROJ
    return pl.pallas_call(
        _ssd_proj_kernel,
        out_shape=jax.ShapeDtypeStruct((b, s, S_PROJ), F32),
        grid=(b, s // tm),
        in_specs=[pl.BlockSpec((None, tm, d), lambda bi, i: (bi, i, 0)), _resident((1, d)), _resident((d, S_PROJ)),
                  _resident((SSD_CONV, S_CONV_DIM)), _resident((1, S_CONV_DIM))],
        out_specs=pl.BlockSpec((None, tm, S_PROJ), lambda bi, i: (bi, i, 0)),
        scratch_shapes=[pltpu.VMEM((HALO_SLABS * SUBLANES, S_CONV_DIM), F32)],
        compiler_params=_compiler_params(("parallel", "arbitrary")),
        name="ssd_proj",
    )(h, nw_row, w_bf16, cw, cb_row)


def _mlp_kernel(x_ref, nw_ref, w1_ref, w2_ref, fnw_ref, o_ref, *, final):
    x = x_ref[...]
    xn = _rmsnorm(x, nw_ref[...], 1e-6).astype(BF16)
    acc = x
    for f in range(w1_ref.shape[1] // FF_CHUNK):
        cols = slice(f * FF_CHUNK, (f + 1) * FF_CHUNK)
        hid = jnp.maximum(jnp.dot(xn, w1_ref[:, cols], preferred_element_type=F32), 0.0)
        acc = acc + jnp.dot((hid * hid).astype(BF16), w2_ref[cols, :], preferred_element_type=F32)
    if final:
        acc = _rmsnorm(acc, fnw_ref[...], 1e-6)
    o_ref[...] = acc


def _mlp(x2d, nw_row, w1, w2, fnw_row, final):
    t, d = x2d.shape
    dff = w1.shape[1]
    return pl.pallas_call(
        functools.partial(_mlp_kernel, final=final),
        out_shape=jax.ShapeDtypeStruct((t, d), F32),
        grid=(t // TM_MLP,),
        in_specs=[pl.BlockSpec((TM_MLP, d), lambda i: (i, 0)), _resident((1, d)), _resident((d, dff)),
                  _resident((dff, d)), _resident((1, d))],
        out_specs=pl.BlockSpec((TM_MLP, d), lambda i: (i, 0)),
        compiler_params=_compiler_params(("parallel",)),
        name="mlp",
    )(x2d, nw_row, w1, w2, fnw_row)


def _shift_tokens(x, d, fill):
    n_slab = CHUNK // SUBLANES
    sub = lax.broadcasted_iota(jnp.int32, (SUBLANES, x.shape[1]), 0)
    slabs = [x[i * SUBLANES:(i + 1) * SUBLANES] for i in range(n_slab)]
    if d < n_slab:
        wrapped = [jnp.where(sub == 0, fill, pltpu.roll(slabs[n_slab - d + i], 1, 0)) for i in range(d)]
        return jnp.concatenate(wrapped + slabs[:n_slab - d], axis=0)
    k = d // n_slab
    return jnp.concatenate([jnp.where(sub < k, fill, pltpu.roll(sl, k, 0)) for sl in slabs], axis=0)


def _token_scan(x, combine, identity):
    d = 1
    while d < CHUNK:
        x = combine(x, _shift_tokens(x, d, identity))
        d *= 2
    return x


def _token_cummax(x):
    return _token_scan(x, jnp.maximum, -jnp.inf)


def _token_cumsum(x):
    return _token_scan(x, jnp.add, 0.0)


def _embed_rows(block, starts):
    piece = block.shape[0] // len(starts)
    parts, pos = [], 0
    for i, start in enumerate(starts):
        if start > pos:
            parts.append(jnp.zeros((start - pos, block.shape[1]), block.dtype))
        parts.append(block[i * piece:(i + 1) * piece])
        pos = start + piece
    if pos < QK_ALL:
        parts.append(jnp.zeros((QK_ALL - pos, block.shape[1]), block.dtype))
    return jnp.concatenate(parts, axis=0)


def _hyb_core_kernel(proj_ref, h_ref, cos_ref, sin_ref, gbias_ref, nw_ref, dec_ref, wq_ref, rrow_ref,
                     wout_ref, o_ref, cst_ref, mst_ref, rst_ref, x_ref, dn_ref, ycat_ref):
    @pl.when(pl.program_id(1) == 0)
    def _init():
        cst_ref[...] = jnp.zeros_like(cst_ref)
        mst_ref[...] = jnp.zeros_like(mst_ref)
        rst_ref[...] = jnp.zeros_like(rst_ref)

    causal = _causal_mask()
    lane_qk = lax.broadcasted_iota(jnp.int32, (1, QK_ALL), 1)
    ones_v = jnp.ones((CHUNK, HEAD_V), BF16)
    neg_inf = jnp.float32(-jnp.inf)
    n_chunks = proj_ref.shape[0] // CHUNK
    n_heads = MLSTM_HEADS + RET_HEADS
    half = HEAD_QK // 2

    def finish_chunk(rows):
        x = x_ref[...]
        d = x - jnp.mean(x, axis=-1, keepdims=True)
        var = jnp.mean(d * d, axis=-1, keepdims=True)
        dn = dn_ref[...]
        eps = jnp.concatenate([1e-5 * dn * dn, jnp.full((RET_HEADS * CHUNK, HEAD_V), 1e-5, F32)], axis=0)
        y = d * lax.rsqrt(var + eps)
        for h in range(n_heads):
            cols = slice(h * HEAD_V, (h + 1) * HEAD_V)
            if h < MLSTM_HEADS:
                gate = _sigmoid(proj_ref[rows, H_MO + h * HEAD_V:H_MO + (h + 1) * HEAD_V])
            else:
                hr = h - MLSTM_HEADS
                gate = _silu(proj_ref[rows, H_RG + hr * HEAD_V:H_RG + (hr + 1) * HEAD_V])
            ycat_ref[rows, cols] = (gate * (y[h * CHUNK:(h + 1) * CHUNK] * nw_ref[:, cols])).astype(BF16)

    def chunk_body(c, carry):
        rows = pl.ds(pl.multiple_of(c * CHUNK, CHUNK), CHUNK)

        gi = proj_ref[rows, H_GI:H_GI + LANES] + gbias_ref[0:1, :]
        gf = proj_ref[rows, H_GF:H_GF + LANES] + gbias_ref[1:2, :]
        bc = _token_cumsum(_log_sigmoid(gf))
        r = gi - bc
        m_prev = mst_ref[0:1, :]
        mu = jnp.maximum(m_prev, _token_cummax(r))
        g_row = bc[CHUNK - 1:CHUNK, :]
        m_new = jnp.maximum(g_row + m_prev, jnp.max(g_row + r, axis=0, keepdims=True))
        s_old = jnp.exp(g_row + m_prev - m_new)
        wt = jnp.exp(g_row + r - m_new).T
        rt = (r - m_prev).T
        mst_ref[0:1, :] = m_new
        alpha = mu - m_prev
        beta = bc + mu

        mq = (proj_ref[rows, H_MQ:H_MQ + QK_ALL] * 0.125).astype(BF16)
        mkt_f = proj_ref[rows, H_MK:H_MK + QK_ALL].T
        qhs = [jnp.where(lane_qk // HEAD_QK == h, mq, jnp.zeros_like(mq)) for h in range(MLSTM_HEADS)]
        scores = jnp.dot(jnp.concatenate(qhs, axis=0), mkt_f.astype(BF16), preferred_element_type=F32)
        for h in range(MLSTM_HEADS):
            hrows = slice(h * CHUNK, (h + 1) * CHUNK)
            alpha_b = _lane_replicate(alpha, h)
            beta_b = _lane_replicate(beta, h)
            c_prev = cst_ref[h]
            vext = jnp.concatenate(
                [proj_ref[rows, H_MV + h * HEAD_V:H_MV + (h + 1) * HEAD_V].astype(BF16), ones_v], axis=1)

            dmat = jnp.exp(jnp.where(causal, rt[h:h + 1, :] - alpha_b, neg_inf))
            s = scores[hrows] * dmat
            kw = (mkt_f[h * HEAD_QK:(h + 1) * HEAD_QK] * wt[h:h + 1, :]).astype(BF16)
            both = jnp.dot(jnp.concatenate([s.astype(BF16), kw], axis=0), vext, preferred_element_type=F32)
            inter = jnp.dot(qhs[h], _embed_rows(c_prev.astype(BF16), [h * HEAD_QK]),
                            preferred_element_type=F32)
            si_b = jnp.exp(-alpha_b)
            num_ext = both[:CHUNK] + jnp.concatenate([si_b, si_b], axis=1) * inter
            x_ref[hrows, :] = num_ext[:, :HEAD_V]
            dn_ref[hrows, :] = jnp.maximum(jnp.abs(num_ext[:, HEAD_V:]), jnp.exp(-beta_b))
            cst_ref[h] = s_old[:, h:h + 1] * c_prev + both[CHUNK:]

        cs, sn = cos_ref[rows, :], sin_ref[rows, :]
        q1, q2 = proj_ref[rows, H_RQ:H_RQ + LANES], proj_ref[rows, H_RQ + LANES:H_RQ + 2 * LANES]
        k1, k2 = proj_ref[rows, H_RK:H_RK + LANES], proj_ref[rows, H_RK + LANES:H_RK + 2 * LANES]
        rq = (jnp.concatenate([q1 * cs - q2 * sn, q1 * sn + q2 * cs], axis=1) * 0.125).astype(BF16)
        rkt_f = jnp.concatenate([k1 * cs - k2 * sn, k1 * sn + k2 * cs], axis=1).T
        qhs = [jnp.where((lane_qk % LANES) // half == h, rq, jnp.zeros_like(rq)) for h in range(RET_HEADS)]
        scores = jnp.dot(jnp.concatenate(qhs, axis=0), rkt_f.astype(BF16), preferred_element_type=F32)
        for h in range(RET_HEADS):
            hrows = slice((MLSTM_HEADS + h) * CHUNK, (MLSTM_HEADS + h + 1) * CHUNK)
            v = proj_ref[rows, H_RV + h * HEAD_V:H_RV + (h + 1) * HEAD_V].astype(BF16)
            r_prev = rst_ref[h]
            s = scores[h * CHUNK:(h + 1) * CHUNK] * dec_ref[h]
            k_own = jnp.concatenate([rkt_f[h * half:(h + 1) * half], rkt_f[LANES + h * half:LANES + (h + 1) * half]],
                                    axis=0)
            kw = (k_own * rrow_ref[h:h + 1, :]).astype(BF16)
            both = jnp.dot(jnp.concatenate([s.astype(BF16), kw], axis=0), v, preferred_element_type=F32)
            inter = jnp.dot(qhs[h], _embed_rows(r_prev.astype(BF16), [h * half, LANES + h * half]),
                            preferred_element_type=F32)
            x_ref[hrows, :] = both[:CHUNK] + wq_ref[h] * inter
            rst_ref[h] = rrow_ref[RET_HEADS + h:RET_HEADS + h + 1, 0:1] * r_prev + both[CHUNK:]

        finish_chunk(rows)
        return carry

    lax.fori_loop(0, n_chunks, chunk_body, 0)
    o_ref[...] = h_ref[...] + jnp.dot(ycat_ref[...], wout_ref[...], preferred_element_type=F32)


def _hyb_core(proj, h, cos_t, sin_t, gbias, nw_row, dec, wq_rep, rrow, wout):
    b, s, d = h.shape
    ts = TS_CORE
    inner = wout.shape[0]
    n_heads = MLSTM_HEADS + RET_HEADS
    return pl.pallas_call(
        _hyb_core_kernel,
        out_shape=jax.ShapeDtypeStruct((b, s, d), F32),
        grid=(b, s // ts),
        in_specs=[
            pl.BlockSpec((None, ts, H_PROJ), lambda bi, i: (bi, i, 0)),
            pl.BlockSpec((None, ts, d), lambda bi, i: (bi, i, 0)),
            pl.BlockSpec((ts, LANES), lambda bi, i: (i, 0)),
            pl.BlockSpec((ts, LANES), lambda bi, i: (i, 0)),
            _resident((SUBLANES, LANES)),
            _resident((1, inner)),
            _resident((RET_HEADS, CHUNK, CHUNK)),
            _resident((RET_HEADS, CHUNK, HEAD_V)),
            _resident((SUBLANES, LANES)),
            _resident((inner, d)),
        ],
        out_specs=pl.BlockSpec((None, ts, d), lambda bi, i: (bi, i, 0)),
        scratch_shapes=[
            pltpu.VMEM((MLSTM_HEADS, HEAD_QK, 2 * HEAD_V), F32),
            pltpu.VMEM((SUBLANES, LANES), F32),
            pltpu.VMEM((RET_HEADS, HEAD_QK, HEAD_V), F32),
            pltpu.VMEM((n_heads * CHUNK, HEAD_V), F32),
            pltpu.VMEM((MLSTM_HEADS * CHUNK, HEAD_V), F32),
            pltpu.VMEM((ts, inner), BF16),
        ],
        compiler_params=_compiler_params(("parallel", "arbitrary")),
        name="hybrid_core",
    )(proj, h, cos_t, sin_t, gbias, nw_row, dec, wq_rep, rrow, wout)


def _ssd_core_kernel(proj_ref, h_ref, dtb_ref, alog_ref, dsk_ref, nw_ref, wout_ref, o_ref, st_ref, ycat_ref):
    @pl.when(pl.program_id(1) == 0)
    def _init():
        st_ref[...] = jnp.zeros_like(st_ref)

    causal = _causal_mask()
    lane_g = lax.broadcasted_iota(jnp.int32, (1, SSD_GW), 1)
    neg_inf = jnp.float32(-jnp.inf)
    a_row = -jnp.exp(alog_ref[...])
    n_chunks = proj_ref.shape[0] // CHUNK

    def chunk_body(c, carry):
        rows = pl.ds(pl.multiple_of(c * CHUNK, CHUNK), CHUNK)

        dtp = _softplus(proj_ref[rows, S_DT:S_DT + LANES] + dtb_ref[...])
        cs = _token_cumsum(dtp * a_row)
        cst = cs.T
        dtt = dtp.T
        toendt = jnp.exp(cst[:, CHUNK - 1:CHUNK] - cst) * dtt
        dec_last = jnp.exp(cs[CHUNK - 1:CHUNK, :])

        for g in range(SSD_GROUPS):
            gcols = slice(g * SSD_GW, (g + 1) * SSD_GW)
            bg = proj_ref[rows, S_B + g * SSD_N:S_B + (g + 1) * SSD_N]
            cg = proj_ref[rows, S_C + g * SSD_N:S_C + (g + 1) * SSD_N]
            xg = proj_ref[rows, S_X + g * SSD_GW:S_X + (g + 1) * SSD_GW]
            cbm = lax.dot_general(cg.astype(BF16), bg.astype(BF16), (((1,), (1,)), ((), ())),
                                  preferred_element_type=F32)
            bgt = bg.T
            xb = xg.astype(BF16)
            st_prev = st_ref[g]
            hb = st_prev.astype(BF16)
            decay_row = jnp.zeros((1, SSD_GW), F32)
            lhs, rhs, st_lhs, st_rhs = [], [], [], []
            for j in range(SSD_REP):
                hh = g * SSD_REP + j
                jmask = lane_g // SSD_P == j
                xm = jnp.where(jmask, xb, jnp.zeros_like(xb))
                csb = _lane_replicate(cs, hh)
                seg = jnp.exp(jnp.where(causal, csb - cst[hh:hh + 1, :], neg_inf))
                lhs += [(cbm * seg * dtt[hh:hh + 1, :]).astype(BF16), (cg * jnp.exp(csb)).astype(BF16)]
                rhs += [xm, jnp.where(jmask, hb, jnp.zeros_like(hb))]
                st_lhs.append((bgt * toendt[hh:hh + 1, :]).astype(BF16))
                st_rhs.append(xm)
                decay_row = jnp.where(jmask, dec_last[:, hh:hh + 1], decay_row)
            yg = dsk_ref[:, gcols] * xg + jnp.dot(jnp.concatenate(lhs, axis=1), jnp.concatenate(rhs, axis=0),
                                                  preferred_element_type=F32)
            st_ref[g] = st_prev * decay_row + jnp.dot(jnp.concatenate(st_lhs, axis=1),
                                                      jnp.concatenate(st_rhs, axis=0), preferred_element_type=F32)
            yz = yg * proj_ref[rows, S_Z + g * SSD_GW:S_Z + (g + 1) * SSD_GW]
            ycat_ref[rows, gcols] = _rmsnorm(yz, nw_ref[:, gcols], 1e-5).astype(BF16)
        return carry

    lax.fori_loop(0, n_chunks, chunk_body, 0)
    o_ref[...] = h_ref[...] + jnp.dot(ycat_ref[...], wout_ref[...], preferred_element_type=F32)


def _ssd_core(proj, h, dtb, alog, dsk, nw_row, wout):
    b, s, d = h.shape
    ts = TS_CORE
    return pl.pallas_call(
        _ssd_core_kernel,
        out_shape=jax.ShapeDtypeStruct((b, s, d), F32),
        grid=(b, s // ts),
        in_specs=[
            pl.BlockSpec((None, ts, S_PROJ), lambda bi, i: (bi, i, 0)),
            pl.BlockSpec((None, ts, d), lambda bi, i: (bi, i, 0)),
            _resident((1, LANES)),
            _resident((1, LANES)),
            _resident((1, SSD_INNER)),
            _resident((1, SSD_INNER)),
            _resident((SSD_INNER, d)),
        ],
        out_specs=pl.BlockSpec((None, ts, d), lambda bi, i: (bi, i, 0)),
        scratch_shapes=[
            pltpu.VMEM((SSD_GROUPS, SSD_N, SSD_GW), F32),
            pltpu.VMEM((ts, SSD_INNER), BF16),
        ],
        compiler_params=_compiler_params(("parallel", "arbitrary")),
        name="ssd_core",
    )(proj, h, dtb, alog, dsk, nw_row, wout)


def _pad_lanes(row, width=LANES):
    return jnp.pad(row, ((0, 0), (0, width - row.shape[1])))


def _hybrid_layer(h, nw, w_in, i_bias, f_bias, m_norm_w, r_norm_w, w_out, cos_t, sin_t, dec, wq_rep, rrow):
    b, s, d = h.shape
    mq, mk, mv, mi, mf, mo, rq, rk, rv, rg = jnp.split(
        w_in, (256, 512, 1024, 1028, 1032, 1544, 1800, 2056, 2568), axis=1)
    perm = jnp.concatenate([
        (jnp.arange(LANES) // 32) * 64 + jnp.arange(LANES) % 32,
        (jnp.arange(LANES) // 32) * 64 + 32 + jnp.arange(LANES) % 32])
    gate_cols = _pad_lanes
    w_cat = jnp.concatenate([mq, mk, mv, mo, rq[:, perm], rk[:, perm], rv, rg, gate_cols(mi), gate_cols(mf)],
                            axis=1).astype(BF16)
    gbias = jnp.pad(jnp.concatenate([gate_cols(i_bias[None, :]), gate_cols(f_bias[None, :])], axis=0),
                    ((0, SUBLANES - 2), (0, 0)))
    proj = _norm_matmul(h.reshape(b * s, d), nw[None, :], w_cat).reshape(b, s, H_PROJ)
    nw_row = jnp.concatenate([m_norm_w, r_norm_w])[None, :]
    return _hyb_core(proj, h, cos_t, sin_t, gbias, nw_row, dec, wq_rep, rrow, w_out.astype(BF16))


def _ssd_layer(h, nw, w_in, conv_w, conv_b, dt_bias, a_log, d_skip, norm_w, w_out):
    b, s, d = h.shape
    w_cat = jnp.pad(w_in, ((0, 0), (0, S_PROJ - w_in.shape[1]))).astype(BF16)
    proj = _ssd_proj(h, nw[None, :], w_cat, conv_w, conv_b[None, :])
    dsk = jnp.repeat(d_skip, SSD_P)[None, :]
    return _ssd_core(proj, h, _pad_lanes(dt_bias[None, :]), _pad_lanes(a_log[None, :]), dsk, norm_w[None, :],
                     w_out.astype(BF16))


def _position_tables(seq):
    dk = 64
    inv = ROPE_BASE ** (-jnp.arange(0, dk, 2, dtype=F32) / dk)
    ang = jnp.arange(seq, dtype=F32)[:, None] * inv[None, :]
    cos_t = jnp.tile(jnp.cos(ang), (1, RET_HEADS))
    sin_t = jnp.tile(jnp.sin(ang), (1, RET_HEADS))
    log_gamma = jnp.log(1.0 - 2.0 ** (-5.0 - jnp.arange(RET_HEADS, dtype=F32)))
    idx = jnp.arange(CHUNK, dtype=F32)
    rel = idx[:, None] - idx[None, :]
    dec = jnp.where((rel >= 0)[None], jnp.exp(jnp.maximum(rel, 0.0)[None] * log_gamma[:, None, None]), 0.0)
    w_q = jnp.exp((idx + 1.0)[:, None] * log_gamma)
    w_k = jnp.exp((CHUNK - 1.0 - idx)[:, None] * log_gamma)
    chunk_decay = jnp.broadcast_to(jnp.exp(CHUNK * log_gamma)[:, None], (RET_HEADS, CHUNK))
    tok = _token_of_row(jnp.arange(CHUNK))
    wq_rep = jnp.broadcast_to(w_q[tok].T[:, :, None], (RET_HEADS, CHUNK, HEAD_V))
    rrow = jnp.concatenate([w_k[tok].T, chunk_decay], axis=0)
    return _interleave_tokens(cos_t, 0), _interleave_tokens(sin_t, 0), dec[:, tok][:, :, tok], wq_rep, rrow


def _interleave_tokens(a, axis):
    shape = a.shape
    split = shape[:axis] + (shape[axis] // CHUNK, SUBLANES, CHUNK // SUBLANES) + shape[axis + 1:]
    return jnp.swapaxes(a.reshape(split), axis + 1, axis + 2).reshape(shape)


def _deinterleave_tokens(a, axis):
    shape = a.shape
    split = shape[:axis] + (shape[axis] // CHUNK, CHUNK // SUBLANES, SUBLANES) + shape[axis + 1:]
    return jnp.swapaxes(a.reshape(split), axis + 1, axis + 2).reshape(shape)


@jax.jit
def kernel(x, norm_mix_w, norm_mlp_w, hyb_w_in, mlstm_i_bias, mlstm_f_bias, mlstm_norm_w, ret_norm_w, hyb_w_out,
           ssd_w_in, ssd_conv_w, ssd_conv_b, ssd_dt_bias, ssd_a_log, ssd_d, ssd_norm_w, ssd_w_out, mlp_w1, mlp_w2,
           final_norm_w):
    b, s, d = x.shape
    depth = norm_mix_w.shape[0]
    cos_t, sin_t, dec, wq_rep, rrow = _position_tables(s)
    h = _interleave_tokens(x, 1)
    for layer in range(depth):
        j = layer // 2
        if layer % 2 == 0:
            h = _hybrid_layer(h, norm_mix_w[layer], hyb_w_in[j], mlstm_i_bias[j], mlstm_f_bias[j], mlstm_norm_w[j],
                              ret_norm_w[j], hyb_w_out[j], cos_t, sin_t, dec, wq_rep, rrow)
        else:
            h = _ssd_layer(h, norm_mix_w[layer], ssd_w_in[j], ssd_conv_w[j], ssd_conv_b[j], ssd_dt_bias[j],
                           ssd_a_log[j], ssd_d[j], ssd_norm_w[j], ssd_w_out[j])
        h = _mlp(h.reshape(b * s, d), norm_mlp_w[layer][None, :], mlp_w1[layer].astype(BF16),
                 mlp_w2[layer].astype(BF16), final_norm_w[None, :], layer == depth - 1).reshape(b, s, d)
    return _deinterleave_tokens(h, 1)
```

```python
import functools

import jax
import jax.numpy as jnp
from jax import lax
from jax.experimental import pallas as pl
from jax.experimental.pallas import tpu as pltpu

F32 = jnp.float32
BF16 = jnp.bfloat16

CHUNK = 128
LANES = 128
SUBLANES = 8
VMEM_LIMIT_BYTES = 56 * 1024 * 1024
ROPE_BASE = 10000.0

MLSTM_HEADS = 4
RET_HEADS = 4
HEAD_QK = 64
QK_ALL = 256
HEAD_V = 128
SSD_GROUPS = 8
SSD_REP = 4
SSD_P = 64
SSD_N = 128
SSD_HEADS = SSD_GROUPS * SSD_REP
SSD_GW = SSD_REP * SSD_P
SSD_INNER = SSD_GROUPS * SSD_GW
SSD_CONV = 4

H_MQ, H_MK, H_MV, H_MO = 0, 256, 512, 1024
H_RQ, H_RK, H_RV, H_RG = 1536, 1792, 2048, 2560
H_GI, H_GF = 3072, 3200
H_PROJ = H_GF + LANES
S_Z, S_X, S_B, S_C, S_DT = 0, 2048, 4096, 5120, 6144
S_PROJ = S_DT + LANES
S_CONV_DIM = S_DT - S_X

TM_PROJ = 512
TM_SSD_PROJ = 256
CONV_COLS = 512
HALO_SLABS = SSD_CONV - 1
TM_MLP = 512
TS_CORE = 512
FF_CHUNK = 1024


def _log1p_exp_neg_abs(x):
    return jnp.log1p(jnp.exp(-jnp.abs(x)))


def _softplus(x):
    return jnp.maximum(x, 0.0) + _log1p_exp_neg_abs(x)


def _log_sigmoid(x):
    return jnp.minimum(x, 0.0) - _log1p_exp_neg_abs(x)


def _sigmoid(x):
    return 0.5 * jnp.tanh(0.5 * x) + 0.5


def _silu(x):
    half = 0.5 * x
    return half * jnp.tanh(half) + half


def _rmsnorm(x, w_row, eps):
    return x * lax.rsqrt(jnp.mean(x * x, axis=-1, keepdims=True) + eps) * w_row


def _token_of_row(r):
    return (r % SUBLANES) * (CHUNK // SUBLANES) + r // SUBLANES


def _causal_mask():
    row = lax.broadcasted_iota(jnp.int32, (CHUNK, CHUNK), 0)
    col = lax.broadcasted_iota(jnp.int32, (CHUNK, CHUNK), 1)
    return _token_of_row(row) >= _token_of_row(col)


def _lane_replicate(x, c):
    return jnp.broadcast_to(x[:, c:c + 1], x.shape)


def _compiler_params(semantics):
    return pltpu.CompilerParams(dimension_semantics=semantics, vmem_limit_bytes=VMEM_LIMIT_BYTES)


def _resident(shape):
    zeros = (0,) * len(shape)
    return pl.BlockSpec(shape, lambda *_: zeros, pipeline_mode=pl.Buffered(1))


def _norm_matmul_kernel(x_ref, nw_ref, w_ref, o_ref):
    xn = _rmsnorm(x_ref[...], nw_ref[...], 1e-6).astype(BF16)
    o_ref[...] = jnp.dot(xn, w_ref[...], preferred_element_type=F32)


def _norm_matmul(x2d, nw_row, w_bf16):
    t, d = x2d.shape
    n = w_bf16.shape[1]
    return pl.pallas_call(
        _norm_matmul_kernel,
        out_shape=jax.ShapeDtypeStruct((t, n), F32),
        grid=(t // TM_PROJ,),
        in_specs=[pl.BlockSpec((TM_PROJ, d), lambda i: (i, 0)), _resident((1, d)), _resident((d, n))],
        out_specs=pl.BlockSpec((TM_PROJ, n), lambda i: (i, 0)),
        compiler_params=_compiler_params(("parallel",)),
        name="norm_matmul",
    )(x2d, nw_row, w_bf16)


def _causal_conv_silu(raw, prev_tail, cw_ref, cb_ref, cc, last_sublane):
    wrapped = []
    for k in range(HALO_SLABS):
        cur = raw[CHUNK - (HALO_SLABS - k) * SUBLANES:CHUNK - (HALO_SLABS - k - 1) * SUBLANES]
        prv = prev_tail[k * SUBLANES:(k + 1) * SUBLANES]
        wrapped.append(pltpu.roll(jnp.where(last_sublane, prv, cur), 1, 0))
    acc = raw * cw_ref[SSD_CONV - 1:SSD_CONV, cc] + cb_ref[:, cc]
    for j in range(1, SSD_CONV):
        shifted = jnp.concatenate(wrapped[HALO_SLABS - j:] + [raw[:CHUNK - j * SUBLANES]], axis=0)
        acc = acc + shifted * cw_ref[SSD_CONV - 1 - j:SSD_CONV - j, cc]
    return _silu(acc)


def _ssd_proj_kernel(x_ref, nw_ref, w_ref, cw_ref, cb_ref, o_ref, halo_ref):
    @pl.when(pl.program_id(1) == 0)
    def _init():
        halo_ref[...] = jnp.zeros_like(halo_ref)

    xn = _rmsnorm(x_ref[...], nw_ref[...], 1e-6).astype(BF16)
    last_sublane = lax.broadcasted_iota(jnp.int32, (SUBLANES, CONV_COLS), 0) == SUBLANES - 1
    n_chunks = x_ref.shape[0] // CHUNK
    tail = slice(CHUNK - HALO_SLABS * SUBLANES, CHUNK)

    for cb in range(S_X // CONV_COLS):
        cols = slice(cb * CONV_COLS, (cb + 1) * CONV_COLS)
        o_ref[:, cols] = _silu(jnp.dot(xn, w_ref[:, cols], preferred_element_type=F32))
    o_ref[:, S_DT:] = jnp.dot(xn, w_ref[:, S_DT:], preferred_element_type=F32)
    for cb in range(S_CONV_DIM // CONV_COLS):
        cc = slice(cb * CONV_COLS, (cb + 1) * CONV_COLS)
        pc = slice(S_X + cb * CONV_COLS, S_X + (cb + 1) * CONV_COLS)
        raw_all = jnp.dot(xn, w_ref[:, pc], preferred_element_type=F32)
        prev_tail = halo_ref[:, cc]
        for c in range(n_chunks):
            raw = raw_all[c * CHUNK:(c + 1) * CHUNK]
            o_ref[c * CHUNK:(c + 1) * CHUNK, pc] = _causal_conv_silu(raw, prev_tail, cw_ref, cb_ref, cc, last_sublane)
            prev_tail = raw[tail]
        halo_ref[:, cc] = prev_tail


def _ssd_proj(h, nw_row, w_bf16, cw, cb_row):
    b, s, d = h.shape
    tm = TM_SSD_PROJ
    return pl.pallas_call(
        _ssd_proj_kernel,
        out_shape=jax.ShapeDtypeStruct((b, s, S_PROJ), F32),
        grid=(b, s // tm),
        in_specs=[pl.BlockSpec((None, tm, d), lambda bi, i: (bi, i, 0)), _resident((1, d)), _resident((d, S_PROJ)),
                  _resident((SSD_CONV, S_CONV_DIM)), _resident((1, S_CONV_DIM))],
        out_specs=pl.BlockSpec((None, tm, S_PROJ), lambda bi, i: (bi, i, 0)),
        scratch_shapes=[pltpu.VMEM((HALO_SLABS * SUBLANES, S_CONV_DIM), F32)],
        compiler_params=_compiler_params(("parallel", "arbitrary")),
        name="ssd_proj",
    )(h, nw_row, w_bf16, cw, cb_row)


def _mlp_kernel(x_ref, nw_ref, w1_ref, w2_ref, fnw_ref, o_ref, *, final):
    x = x_ref[...]
    xn = _rmsnorm(x, nw_ref[...], 1e-6).astype(BF16)
    acc = x
    for f in range(w1_ref.shape[1] // FF_CHUNK):
        cols = slice(f * FF_CHUNK, (f + 1) * FF_CHUNK)
        hid = jnp.maximum(jnp.dot(xn, w1_ref[:, cols], preferred_element_type=F32), 0.0)
        acc = acc + jnp.dot((hid * hid).astype(BF16), w2_ref[cols, :], preferred_element_type=F32)
    if final:
        acc = _rmsnorm(acc, fnw_ref[...], 1e-6)
    o_ref[...] = acc


def _mlp(x2d, nw_row, w1, w2, fnw_row, final):
    t, d = x2d.shape
    dff = w1.shape[1]
    return pl.pallas_call(
        functools.partial(_mlp_kernel, final=final),
        out_shape=jax.ShapeDtypeStruct((t, d), F32),
        grid=(t // TM_MLP,),
        in_specs=[pl.BlockSpec((TM_MLP, d), lambda i: (i, 0)), _resident((1, d)), _resident((d, dff)),
                  _resident((dff, d)), _resident((1, d))],
        out_specs=pl.BlockSpec((TM_MLP, d), lambda i: (i, 0)),
        compiler_params=_compiler_params(("parallel",)),
        name="mlp",
    )(x2d, nw_row, w1, w2, fnw_row)


def _shift_tokens(x, d, fill):
    n_slab = CHUNK // SUBLANES
    sub = lax.broadcasted_iota(jnp.int32, (SUBLANES, x.shape[1]), 0)
    slabs = [x[i * SUBLANES:(i + 1) * SUBLANES] for i in range(n_slab)]
    if d < n_slab:
        wrapped = [jnp.where(sub == 0, fill, pltpu.roll(slabs[n_slab - d + i], 1, 0)) for i in range(d)]
        return jnp.concatenate(wrapped + slabs[:n_slab - d], axis=0)
    k = d // n_slab
    return jnp.concatenate([jnp.where(sub < k, fill, pltpu.roll(sl, k, 0)) for sl in slabs], axis=0)


def _token_scan(x, combine, identity):
    d = 1
    while d < CHUNK:
        x = combine(x, _shift_tokens(x, d, identity))
        d *= 2
    return x


def _token_cummax(x):
    return _token_scan(x, jnp.maximum, -jnp.inf)


def _token_cumsum(x):
    return _token_scan(x, jnp.add, 0.0)


def _embed_rows(block, starts):
    piece = block.shape[0] // len(starts)
    parts, pos = [], 0
    for i, start in enumerate(starts):
        if start > pos:
            parts.append(jnp.zeros((start - pos, block.shape[1]), block.dtype))
        parts.append(block[i * piece:(i + 1) * piece])
        pos = start + piece
    if pos < QK_ALL:
        parts.append(jnp.zeros((QK_ALL - pos, block.shape[1]), block.dtype))
    return jnp.concatenate(parts, axis=0)


def _hyb_core_kernel(proj_ref, h_ref, cos_ref, sin_ref, gbias_ref, nw_ref, dec_ref, wq_ref, rrow_ref,
                     wout_ref, o_ref, cst_ref, mst_ref, rst_ref, x_ref, dn_ref, ycat_ref):
    @pl.when(pl.program_id(1) == 0)
    def _init():
        cst_ref[...] = jnp.zeros_like(cst_ref)
        mst_ref[...] = jnp.zeros_like(mst_ref)
        rst_ref[...] = jnp.zeros_like(rst_ref)

    causal = _causal_mask()
    lane_qk = lax.broadcasted_iota(jnp.int32, (1, QK_ALL), 1)
    ones_v = jnp.ones((CHUNK, HEAD_V), BF16)
    neg_inf = jnp.float32(-jnp.inf)
    n_chunks = proj_ref.shape[0] // CHUNK
    n_heads = MLSTM_HEADS + RET_HEADS
    half = HEAD_QK // 2

    def finish_chunk(rows):
        x = x_ref[...]
        d = x - jnp.mean(x, axis=-1, keepdims=True)
        var = jnp.mean(d * d, axis=-1, keepdims=True)
        dn = dn_ref[...]
        eps = jnp.concatenate([1e-5 * dn * dn, jnp.full((RET_HEADS * CHUNK, HEAD_V), 1e-5, F32)], axis=0)
        y = d * lax.rsqrt(var + eps)
        for h in range(n_heads):
            cols = slice(h * HEAD_V, (h + 1) * HEAD_V)
            if h < MLSTM_HEADS:
                gate = _sigmoid(proj_ref[rows, H_MO + h * HEAD_V:H_MO + (h + 1) * HEAD_V])
            else:
                hr = h - MLSTM_HEADS
                gate = _silu(proj_ref[rows, H_RG + hr * HEAD_V:H_RG + (hr + 1) * HEAD_V])
            ycat_ref[rows, cols] = (gate * (y[h * CHUNK:(h + 1) * CHUNK] * nw_ref[:, cols])).astype(BF16)

    def chunk_body(c, carry):
        rows = pl.ds(pl.multiple_of(c * CHUNK, CHUNK), CHUNK)

        gi = proj_ref[rows, H_GI:H_GI + LANES] + gbias_ref[0:1, :]
        gf = proj_ref[rows, H_GF:H_GF + LANES] + gbias_ref[1:2, :]
        bc = _token_cumsum(_log_sigmoid(gf))
        r = gi - bc
        m_prev = mst_ref[0:1, :]
        mu = jnp.maximum(m_prev, _token_cummax(r))
        g_row = bc[CHUNK - 1:CHUNK, :]
        m_new = jnp.maximum(g_row + m_prev, jnp.max(g_row + r, axis=0, keepdims=True))
        s_old = jnp.exp(g_row + m_prev - m_new)
        wt = jnp.exp(g_row + r - m_new).T
        rt = (r - m_prev).T
        mst_ref[0:1, :] = m_new
        alpha = mu - m_prev
        beta = bc + mu

        mq = (proj_ref[rows, H_MQ:H_MQ + QK_ALL] * 0.125).astype(BF16)
        mkt_f = proj_ref[rows, H_MK:H_MK + QK_ALL].T
        qhs = [jnp.where(lane_qk // HEAD_QK == h, mq, jnp.zeros_like(mq)) for h in range(MLSTM_HEADS)]
        scores = jnp.dot(jnp.concatenate(qhs, axis=0), mkt_f.astype(BF16), preferred_element_type=F32)
        for h in range(MLSTM_HEADS):
            hrows = slice(h * CHUNK, (h + 1) * CHUNK)
            alpha_b = _lane_replicate(alpha, h)
            beta_b = _lane_replicate(beta, h)
            c_prev = cst_ref[h]
            vext = jnp.concatenate(
                [proj_ref[rows, H_MV + h * HEAD_V:H_MV + (h + 1) * HEAD_V].astype(BF16), ones_v], axis=1)

            dmat = jnp.exp(jnp.where(causal, rt[h:h + 1, :] - alpha_b, neg_inf))
            s = scores[hrows] * dmat
            kw = (mkt_f[h * HEAD_QK:(h + 1) * HEAD_QK] * wt[h:h + 1, :]).astype(BF16)
            both = jnp.dot(jnp.concatenate([s.astype(BF16), kw], axis=0), vext, preferred_element_type=F32)
            inter = jnp.dot(qhs[h], _embed_rows(c_prev.astype(BF16), [h * HEAD_QK]),
                            preferred_element_type=F32)
            si_b = jnp.exp(-alpha_b)
            num_ext = both[:CHUNK] + jnp.concatenate([si_b, si_b], axis=1) * inter
            x_ref[hrows, :] = num_ext[:, :HEAD_V]
            dn_ref[hrows, :] = jnp.maximum(jnp.abs(num_ext[:, HEAD_V:]), jnp.exp(-beta_b))
            cst_ref[h] = s_old[:, h:h + 1] * c_prev + both[CHUNK:]

        cs, sn = cos_ref[rows, :], sin_ref[rows, :]
        q1, q2 = proj_ref[rows, H_RQ:H_RQ + LANES], proj_ref[rows, H_RQ + LANES:H_RQ + 2 * LANES]
        k1, k2 = proj_ref[rows, H_RK:H_RK + LANES], proj_ref[rows, H_RK + LANES:H_RK + 2 * LANES]
        rq = (jnp.concatenate([q1 * cs - q2 * sn, q1 * sn + q2 * cs], axis=1) * 0.125).astype(BF16)
        rkt_f = jnp.concatenate([k1 * cs - k2 * sn, k1 * sn + k2 * cs], axis=1).T
        qhs = [jnp.where((lane_qk % LANES) // half == h, rq, jnp.zeros_like(rq)) for h in range(RET_HEADS)]
        scores = jnp.dot(jnp.concatenate(qhs, axis=0), rkt_f.astype(BF16), preferred_element_type=F32)
        for h in range(RET_HEADS):
            hrows = slice((MLSTM_HEADS + h) * CHUNK, (MLSTM_HEADS + h + 1) * CHUNK)
            v = proj_ref[rows, H_RV + h * HEAD_V:H_RV + (h + 1) * HEAD_V].astype(BF16)
            r_prev = rst_ref[h]
            s = scores[h * CHUNK:(h + 1) * CHUNK] * dec_ref[h]
            k_own = jnp.concatenate([rkt_f[h * half:(h + 1) * half], rkt_f[LANES + h * half:LANES + (h + 1) * half]],
                                    axis=0)
            kw = (k_own * rrow_ref[h:h + 1, :]).astype(BF16)
            both = jnp.dot(jnp.concatenate([s.astype(BF16), kw], axis=0), v, preferred_element_type=F32)
            inter = jnp.dot(qhs[h], _embed_rows(r_prev.astype(BF16), [h * half, LANES + h * half]),
                            preferred_element_type=F32)
            x_ref[hrows, :] = both[:CHUNK] + wq_ref[h] * inter
            rst_ref[h] = rrow_ref[RET_HEADS + h:RET_HEADS + h + 1, 0:1] * r_prev + both[CHUNK:]

        finish_chunk(rows)
        return carry

    lax.fori_loop(0, n_chunks, chunk_body, 0, unroll=4)
    o_ref[...] = h_ref[...] + jnp.dot(ycat_ref[...], wout_ref[...], preferred_element_type=F32)


def _hyb_core(proj, h, cos_t, sin_t, gbias, nw_row, dec, wq_rep, rrow, wout):
    b, s, d = h.shape
    ts = TS_CORE
    inner = wout.shape[0]
    n_heads = MLSTM_HEADS + RET_HEADS
    return pl.pallas_call(
        _hyb_core_kernel,
        out_shape=jax.ShapeDtypeStruct((b, s, d), F32),
        grid=(b, s // ts),
        in_specs=[
            pl.BlockSpec((None, ts, H_PROJ), lambda bi, i: (bi, i, 0)),
            pl.BlockSpec((None, ts, d), lambda bi, i: (bi, i, 0)),
            pl.BlockSpec((ts, LANES), lambda bi, i: (i, 0)),
            pl.BlockSpec((ts, LANES), lambda bi, i: (i, 0)),
            _resident((SUBLANES, LANES)),
            _resident((1, inner)),
            _resident((RET_HEADS, CHUNK, CHUNK)),
            _resident((RET_HEADS, CHUNK, HEAD_V)),
            _resident((SUBLANES, LANES)),
            _resident((inner, d)),
        ],
        out_specs=pl.BlockSpec((None, ts, d), lambda bi, i: (bi, i, 0)),
        scratch_shapes=[
            pltpu.VMEM((MLSTM_HEADS, HEAD_QK, 2 * HEAD_V), F32),
            pltpu.VMEM((SUBLANES, LANES), F32),
            pltpu.VMEM((RET_HEADS, HEAD_QK, HEAD_V), F32),
            pltpu.VMEM((n_heads * CHUNK, HEAD_V), F32),
            pltpu.VMEM((MLSTM_HEADS * CHUNK, HEAD_V), F32),
            pltpu.VMEM((ts, inner), BF16),
        ],
        compiler_params=_compiler_params(("parallel", "arbitrary")),
        name="hybrid_core",
    )(proj, h, cos_t, sin_t, gbias, nw_row, dec, wq_rep, rrow, wout)


def _ssd_core_kernel(proj_ref, h_ref, dtb_ref, alog_ref, dsk_ref, nw_ref, wout_ref, o_ref, st_ref, ycat_ref):
    @pl.when(pl.program_id(1) == 0)
    def _init():
        st_ref[...] = jnp.zeros_like(st_ref)

    causal = _causal_mask()
    lane_g = lax.broadcasted_iota(jnp.int32, (1, SSD_GW), 1)
    neg_inf = jnp.float32(-jnp.inf)
    a_row = -jnp.exp(alog_ref[...])
    n_chunks = proj_ref.shape[0] // CHUNK

    def chunk_body(c, carry):
        rows = pl.ds(pl.multiple_of(c * CHUNK, CHUNK), CHUNK)

        dtp = _softplus(proj_ref[rows, S_DT:S_DT + LANES] + dtb_ref[...])
        cs = _token_cumsum(dtp * a_row)
        cst = cs.T
        dtt = dtp.T
        toendt = jnp.exp(cst[:, CHUNK - 1:CHUNK] - cst) * dtt
        dec_last = jnp.exp(cs[CHUNK - 1:CHUNK, :])

        for g in range(SSD_GROUPS):
            gcols = slice(g * SSD_GW, (g + 1) * SSD_GW)
            bg = proj_ref[rows, S_B + g * SSD_N:S_B + (g + 1) * SSD_N]
            cg = proj_ref[rows, S_C + g * SSD_N:S_C + (g + 1) * SSD_N]
            xg = proj_ref[rows, S_X + g * SSD_GW:S_X + (g + 1) * SSD_GW]
            cbm = lax.dot_general(cg.astype(BF16), bg.astype(BF16), (((1,), (1,)), ((), ())),
                                  preferred_element_type=F32)
            bgt = bg.T
            xb = xg.astype(BF16)
            st_prev = st_ref[g]
            hb = st_prev.astype(BF16)
            decay_row = jnp.zeros((1, SSD_GW), F32)
            lhs, rhs, st_lhs, st_rhs = [], [], [], []
            for j in range(SSD_REP):
                hh = g * SSD_REP + j
                jmask = lane_g // SSD_P == j
                xm = jnp.where(jmask, xb, jnp.zeros_like(xb))
                csb = _lane_replicate(cs, hh)
                seg = jnp.exp(jnp.where(causal, csb - cst[hh:hh + 1, :], neg_inf))
                lhs += [(cbm * seg * dtt[hh:hh + 1, :]).astype(BF16), (cg * jnp.exp(csb)).astype(BF16)]
                rhs += [xm, jnp.where(jmask, hb, jnp.zeros_like(hb))]
                st_lhs.append((bgt * toendt[hh:hh + 1, :]).astype(BF16))
                st_rhs.append(xm)
                decay_row = jnp.where(jmask, dec_last[:, hh:hh + 1], decay_row)
            yg = dsk_ref[:, gcols] * xg + jnp.dot(jnp.concatenate(lhs, axis=1), jnp.concatenate(rhs, axis=0),
                                                  preferred_element_type=F32)
            st_ref[g] = st_prev * decay_row + jnp.dot(jnp.concatenate(st_lhs, axis=1),
                                                      jnp.concatenate(st_rhs, axis=0), preferred_element_type=F32)
            yz = yg * proj_ref[rows, S_Z + g * SSD_GW:S_Z + (g + 1) * SSD_GW]
            ycat_ref[rows, gcols] = _rmsnorm(yz, nw_ref[:, gcols], 1e-5).astype(BF16)
        return carry

    lax.fori_loop(0, n_chunks, chunk_body, 0)
    o_ref[...] = h_ref[...] + jnp.dot(ycat_ref[...], wout_ref[...], preferred_element_type=F32)


def _ssd_core(proj, h, dtb, alog, dsk, nw_row, wout):
    b, s, d = h.shape
    ts = TS_CORE
    return pl.pallas_call(
        _ssd_core_kernel,
        out_shape=jax.ShapeDtypeStruct((b, s, d), F32),
        grid=(b, s // ts),
        in_specs=[
            pl.BlockSpec((None, ts, S_PROJ), lambda bi, i: (bi, i, 0)),
            pl.BlockSpec((None, ts, d), lambda bi, i: (bi, i, 0)),
            _resident((1, LANES)),
            _resident((1, LANES)),
            _resident((1, SSD_INNER)),
            _resident((1, SSD_INNER)),
            _resident((SSD_INNER, d)),
        ],
        out_specs=pl.BlockSpec((None, ts, d), lambda bi, i: (bi, i, 0)),
        scratch_shapes=[
            pltpu.VMEM((SSD_GROUPS, SSD_N, SSD_GW), F32),
            pltpu.VMEM((ts, SSD_INNER), BF16),
        ],
        compiler_params=_compiler_params(("parallel", "arbitrary")),
        name="ssd_core",
    )(proj, h, dtb, alog, dsk, nw_row, wout)


def _pad_lanes(row, width=LANES):
    return jnp.pad(row, ((0, 0), (0, width - row.shape[1])))


def _hybrid_layer(h, nw, w_in, i_bias, f_bias, m_norm_w, r_norm_w, w_out, cos_t, sin_t, dec, wq_rep, rrow):
    b, s, d = h.shape
    mq, mk, mv, mi, mf, mo, rq, rk, rv, rg = jnp.split(
        w_in, (256, 512, 1024, 1028, 1032, 1544, 1800, 2056, 2568), axis=1)
    perm = jnp.concatenate([
        (jnp.arange(LANES) // 32) * 64 + jnp.arange(LANES) % 32,
        (jnp.arange(LANES) // 32) * 64 + 32 + jnp.arange(LANES) % 32])
    gate_cols = _pad_lanes
    w_cat = jnp.concatenate([mq, mk, mv, mo, rq[:, perm], rk[:, perm], rv, rg, gate_cols(mi), gate_cols(mf)],
                            axis=1).astype(BF16)
    gbias = jnp.pad(jnp.concatenate([gate_cols(i_bias[None, :]), gate_cols(f_bias[None, :])], axis=0),
                    ((0, SUBLANES - 2), (0, 0)))
    proj = _norm_matmul(h.reshape(b * s, d), nw[None, :], w_cat).reshape(b, s, H_PROJ)
    nw_row = jnp.concatenate([m_norm_w, r_norm_w])[None, :]
    return _hyb_core(proj, h, cos_t, sin_t, gbias, nw_row, dec, wq_rep, rrow, w_out.astype(BF16))


def _ssd_layer(h, nw, w_in, conv_w, conv_b, dt_bias, a_log, d_skip, norm_w, w_out):
    b, s, d = h.shape
    w_cat = jnp.pad(w_in, ((0, 0), (0, S_PROJ - w_in.shape[1]))).astype(BF16)
    proj = _ssd_proj(h, nw[None, :], w_cat, conv_w, conv_b[None, :])
    dsk = jnp.repeat(d_skip, SSD_P)[None, :]
    return _ssd_core(proj, h, _pad_lanes(dt_bias[None, :]), _pad_lanes(a_log[None, :]), dsk, norm_w[None, :],
                     w_out.astype(BF16))


def _position_tables(seq):
    dk = 64
    inv = ROPE_BASE ** (-jnp.arange(0, dk, 2, dtype=F32) / dk)
    ang = jnp.arange(seq, dtype=F32)[:, None] * inv[None, :]
    cos_t = jnp.tile(jnp.cos(ang), (1, RET_HEADS))
    sin_t = jnp.tile(jnp.sin(ang), (1, RET_HEADS))
    log_gamma = jnp.log(1.0 - 2.0 ** (-5.0 - jnp.arange(RET_HEADS, dtype=F32)))
    idx = jnp.arange(CHUNK, dtype=F32)
    rel = idx[:, None] - idx[None, :]
    dec = jnp.where((rel >= 0)[None], jnp.exp(jnp.maximum(rel, 0.0)[None] * log_gamma[:, None, None]), 0.0)
    w_q = jnp.exp((idx + 1.0)[:, None] * log_gamma)
    w_k = jnp.exp((CHUNK - 1.0 - idx)[:, None] * log_gamma)
    chunk_decay = jnp.broadcast_to(jnp.exp(CHUNK * log_gamma)[:, None], (RET_HEADS, CHUNK))
    tok = _token_of_row(jnp.arange(CHUNK))
    wq_rep = jnp.broadcast_to(w_q[tok].T[:, :, None], (RET_HEADS, CHUNK, HEAD_V))
    rrow = jnp.concatenate([w_k[tok].T, chunk_decay], axis=0)
    return _interleave_tokens(cos_t, 0), _interleave_tokens(sin_t, 0), dec[:, tok][:, :, tok], wq_rep, rrow


def _interleave_tokens(a, axis):
    shape = a.shape
    split = shape[:axis] + (shape[axis] // CHUNK, SUBLANES, CHUNK // SUBLANES) + shape[axis + 1:]
    return jnp.swapaxes(a.reshape(split), axis + 1, axis + 2).reshape(shape)


def _deinterleave_tokens(a, axis):
    shape = a.shape
    split = shape[:axis] + (shape[axis] // CHUNK, CHUNK // SUBLANES, SUBLANES) + shape[axis + 1:]
    return jnp.swapaxes(a.reshape(split), axis + 1, axis + 2).reshape(shape)


@jax.jit
def kernel(x, norm_mix_w, norm_mlp_w, hyb_w_in, mlstm_i_bias, mlstm_f_bias, mlstm_norm_w, ret_norm_w, hyb_w_out,
           ssd_w_in, ssd_conv_w, ssd_conv_b, ssd_dt_bias, ssd_a_log, ssd_d, ssd_norm_w, ssd_w_out, mlp_w1, mlp_w2,
           final_norm_w):
    b, s, d = x.shape
    depth = norm_mix_w.shape[0]
    cos_t, sin_t, dec, wq_rep, rrow = _position_tables(s)
    h = _interleave_tokens(x, 1)
    for layer in range(depth):
        j = layer // 2
        if layer % 2 == 0:
            h = _hybrid_layer(h, norm_mix_w[layer], hyb_w_in[j], mlstm_i_bias[j], mlstm_f_bias[j], mlstm_norm_w[j],
                              ret_norm_w[j], hyb_w_out[j], cos_t, sin_t, dec, wq_rep, rrow)
        else:
            h = _ssd_layer(h, norm_mix_w[layer], ssd_w_in[j], ssd_conv_w[j], ssd_conv_b[j], ssd_dt_bias[j],
                           ssd_a_log[j], ssd_d[j], ssd_norm_w[j], ssd_w_out[j])
        h = _mlp(h.reshape(b * s, d), norm_mlp_w[layer][None, :], mlp_w1[layer].astype(BF16),
                 mlp_w2[layer].astype(BF16), final_norm_w[None, :], layer == depth - 1).reshape(b, s, d)
    return _deinterleave_tokens(h, 1)
```

```python
import functools

import jax
import jax.numpy as jnp
from jax import lax
from jax.experimental import pallas as pl
from jax.experimental.pallas import tpu as pltpu

F32 = jnp.float32
BF16 = jnp.bfloat16

CHUNK = 128
LANES = 128
SUBLANES = 8
VMEM_LIMIT_BYTES = 56 * 1024 * 1024
ROPE_BASE = 10000.0

MLSTM_HEADS = 4
RET_HEADS = 4
HEAD_QK = 64
QK_ALL = 256
HEAD_V = 128
SSD_GROUPS = 8
SSD_REP = 4
SSD_P = 64
SSD_N = 128
SSD_HEADS = SSD_GROUPS * SSD_REP
SSD_GW = SSD_REP * SSD_P
SSD_INNER = SSD_GROUPS * SSD_GW
SSD_CONV = 4

H_MQ, H_MK, H_MV, H_MO = 0, 256, 512, 1024
H_RQ, H_RK, H_RV, H_RG = 1536, 1792, 2048, 2560
H_GI, H_GF = 3072, 3200
H_PROJ = H_GF + LANES
S_Z, S_X, S_B, S_C, S_DT = 0, 2048, 4096, 5120, 6144
S_PROJ = S_DT + LANES
S_CONV_DIM = S_DT - S_X

TM_PROJ = 512
TM_SSD_PROJ = 256
CONV_COLS = 512
HALO_SLABS = SSD_CONV - 1
TM_MLP = 512
TS_CORE = 512
FF_CHUNK = 1024


def _log1p_exp_neg_abs(x):
    return jnp.log1p(jnp.exp(-jnp.abs(x)))


def _softplus(x):
    return jnp.maximum(x, 0.0) + _log1p_exp_neg_abs(x)


def _log_sigmoid(x):
    return jnp.minimum(x, 0.0) - _log1p_exp_neg_abs(x)


def _sigmoid(x):
    return 0.5 * jnp.tanh(0.5 * x) + 0.5


def _silu(x):
    half = 0.5 * x
    return half * jnp.tanh(half) + half


def _rmsnorm(x, w_row, eps):
    return x * lax.rsqrt(jnp.mean(x * x, axis=-1, keepdims=True) + eps) * w_row


def _token_of_row(r):
    return (r % SUBLANES) * (CHUNK // SUBLANES) + r // SUBLANES


def _causal_mask():
    row = lax.broadcasted_iota(jnp.int32, (CHUNK, CHUNK), 0)
    col = lax.broadcasted_iota(jnp.int32, (CHUNK, CHUNK), 1)
    return _token_of_row(row) >= _token_of_row(col)


def _lane_replicate(x, c):
    return jnp.broadcast_to(x[:, c:c + 1], x.shape)


def _compiler_params(semantics):
    return pltpu.CompilerParams(dimension_semantics=semantics, vmem_limit_bytes=VMEM_LIMIT_BYTES)


def _resident(shape):
    zeros = (0,) * len(shape)
    return pl.BlockSpec(shape, lambda *_: zeros, pipeline_mode=pl.Buffered(1))


def _norm_matmul_kernel(x_ref, nw_ref, w_ref, o_ref):
    xn = _rmsnorm(x_ref[...], nw_ref[...], 1e-6).astype(BF16)
    o_ref[...] = jnp.dot(xn, w_ref[...], preferred_element_type=F32)


def _norm_matmul(x2d, nw_row, w_bf16):
    t, d = x2d.shape
    n = w_bf16.shape[1]
    return pl.pallas_call(
        _norm_matmul_kernel,
        out_shape=jax.ShapeDtypeStruct((t, n), F32),
        grid=(t // TM_PROJ,),
        in_specs=[pl.BlockSpec((TM_PROJ, d), lambda i: (i, 0)), _resident((1, d)), _resident((d, n))],
        out_specs=pl.BlockSpec((TM_PROJ, n), lambda i: (i, 0)),
        compiler_params=_compiler_params(("parallel",)),
        name="norm_matmul",
    )(x2d, nw_row, w_bf16)


def _causal_conv_silu(raw, prev_tail, cw_ref, cb_ref, cc, last_sublane):
    wrapped = []
    for k in range(HALO_SLABS):
        cur = raw[CHUNK - (HALO_SLABS - k) * SUBLANES:CHUNK - (HALO_SLABS - k - 1) * SUBLANES]
        prv = prev_tail[k * SUBLANES:(k + 1) * SUBLANES]
        wrapped.append(pltpu.roll(jnp.where(last_sublane, prv, cur), 1, 0))
    acc = raw * cw_ref[SSD_CONV - 1:SSD_CONV, cc] + cb_ref[:, cc]
    for j in range(1, SSD_CONV):
        shifted = jnp.concatenate(wrapped[HALO_SLABS - j:] + [raw[:CHUNK - j * SUBLANES]], axis=0)
        acc = acc + shifted * cw_ref[SSD_CONV - 1 - j:SSD_CONV - j, cc]
    return _silu(acc)


def _ssd_proj_kernel(x_ref, nw_ref, w_ref, cw_ref, cb_ref, o_ref, halo_ref):
    @pl.when(pl.program_id(1) == 0)
    def _init():
        halo_ref[...] = jnp.zeros_like(halo_ref)

    xn = _rmsnorm(x_ref[...], nw_ref[...], 1e-6).astype(BF16)
    last_sublane = lax.broadcasted_iota(jnp.int32, (SUBLANES, CONV_COLS), 0) == SUBLANES - 1
    n_chunks = x_ref.shape[0] // CHUNK
    tail = slice(CHUNK - HALO_SLABS * SUBLANES, CHUNK)

    for cb in range(S_X // CONV_COLS):
        cols = slice(cb * CONV_COLS, (cb + 1) * CONV_COLS)
        o_ref[:, cols] = jnp.dot(xn, w_ref[:, cols], preferred_element_type=F32)
    o_ref[:, S_DT:] = jnp.dot(xn, w_ref[:, S_DT:], preferred_element_type=F32)
    for cb in range(S_CONV_DIM // CONV_COLS):
        cc = slice(cb * CONV_COLS, (cb + 1) * CONV_COLS)
        pc = slice(S_X + cb * CONV_COLS, S_X + (cb + 1) * CONV_COLS)
        raw_all = jnp.dot(xn, w_ref[:, pc], preferred_element_type=F32)
        prev_tail = halo_ref[:, cc]
        for c in range(n_chunks):
            raw = raw_all[c * CHUNK:(c + 1) * CHUNK]
            o_ref[c * CHUNK:(c + 1) * CHUNK, pc] = _causal_conv_silu(raw, prev_tail, cw_ref, cb_ref, cc, last_sublane)
            prev_tail = raw[tail]
        halo_ref[:, cc] = prev_tail


def _ssd_proj(h, nw_row, w_bf16, cw, cb_row):
    b, s, d = h.shape
    tm = TM_SSD_PROJ
    return pl.pallas_call(
        _ssd_proj_kernel,
        out_shape=jax.ShapeDtypeStruct((b, s, S_PROJ), F32),
        grid=(b, s // tm),
        in_specs=[pl.BlockSpec((None, tm, d), lambda bi, i: (bi, i, 0)), _resident((1, d)), _resident((d, S_PROJ)),
                  _resident((SSD_CONV, S_CONV_DIM)), _resident((1, S_CONV_DIM))],
        out_specs=pl.BlockSpec((None, tm, S_PROJ), lambda bi, i: (bi, i, 0)),
        scratch_shapes=[pltpu.VMEM((HALO_SLABS * SUBLANES, S_CONV_DIM), F32)],
        compiler_params=_compiler_params(("parallel", "arbitrary")),
        name="ssd_proj",
    )(h, nw_row, w_bf16, cw, cb_row)


def _mlp_kernel(x_ref, nw_ref, w1_ref, w2_ref, fnw_ref, o_ref, *, final):
    x = x_ref[...]
    xn = _rmsnorm(x, nw_ref[...], 1e-6).astype(BF16)
    acc = x
    for f in range(w1_ref.shape[1] // FF_CHUNK):
        cols = slice(f * FF_CHUNK, (f + 1) * FF_CHUNK)
        hid = jnp.maximum(jnp.dot(xn, w1_ref[:, cols], preferred_element_type=F32), 0.0)
        acc = acc + jnp.dot((hid * hid).astype(BF16), w2_ref[cols, :], preferred_element_type=F32)
    if final:
        acc = _rmsnorm(acc, fnw_ref[...], 1e-6)
    o_ref[...] = acc


def _mlp(x2d, nw_row, w1, w2, fnw_row, final):
    t, d = x2d.shape
    dff = w1.shape[1]
    return pl.pallas_call(
        functools.partial(_mlp_kernel, final=final),
        out_shape=jax.ShapeDtypeStruct((t, d), F32),
        grid=(t // TM_MLP,),
        in_specs=[pl.BlockSpec((TM_MLP, d), lambda i: (i, 0)), _resident((1, d)), _resident((d, dff)),
                  _resident((dff, d)), _resident((1, d))],
        out_specs=pl.BlockSpec((TM_MLP, d), lambda i: (i, 0)),
        compiler_params=_compiler_params(("parallel",)),
        name="mlp",
    )(x2d, nw_row, w1, w2, fnw_row)


def _shift_tokens(x, d, fill):
    n_slab = CHUNK // SUBLANES
    sub = lax.broadcasted_iota(jnp.int32, (SUBLANES, x.shape[1]), 0)
    slabs = [x[i * SUBLANES:(i + 1) * SUBLANES] for i in range(n_slab)]
    if d < n_slab:
        wrapped = [jnp.where(sub == 0, fill, pltpu.roll(slabs[n_slab - d + i], 1, 0)) for i in range(d)]
        return jnp.concatenate(wrapped + slabs[:n_slab - d], axis=0)
    k = d // n_slab
    return jnp.concatenate([jnp.where(sub < k, fill, pltpu.roll(sl, k, 0)) for sl in slabs], axis=0)


def _token_scan(x, combine, identity):
    d = 1
    while d < CHUNK:
        x = combine(x, _shift_tokens(x, d, identity))
        d *= 2
    return x


def _token_cummax(x):
    return _token_scan(x, jnp.maximum, -jnp.inf)


def _token_cumsum(x):
    return _token_scan(x, jnp.add, 0.0)


def _embed_rows(block, starts):
    piece = block.shape[0] // len(starts)
    parts, pos = [], 0
    for i, start in enumerate(starts):
        if start > pos:
            parts.append(jnp.zeros((start - pos, block.shape[1]), block.dtype))
        parts.append(block[i * piece:(i + 1) * piece])
        pos = start + piece
    if pos < QK_ALL:
        parts.append(jnp.zeros((QK_ALL - pos, block.shape[1]), block.dtype))
    return jnp.concatenate(parts, axis=0)


def _hyb_core_kernel(proj_ref, h_ref, cos_ref, sin_ref, gbias_ref, nw_ref, dec_ref, wq_ref, rrow_ref,
                     wout_ref, o_ref, cst_ref, mst_ref, rst_ref, x_ref, dn_ref, ycat_ref):
    @pl.when(pl.program_id(1) == 0)
    def _init():
        cst_ref[...] = jnp.zeros_like(cst_ref)
        mst_ref[...] = jnp.zeros_like(mst_ref)
        rst_ref[...] = jnp.zeros_like(rst_ref)

    causal = _causal_mask()
    lane_qk = lax.broadcasted_iota(jnp.int32, (1, QK_ALL), 1)
    ones_v = jnp.ones((CHUNK, HEAD_V), BF16)
    neg_inf = jnp.float32(-jnp.inf)
    n_chunks = proj_ref.shape[0] // CHUNK
    n_heads = MLSTM_HEADS + RET_HEADS
    half = HEAD_QK // 2

    def finish_chunk(rows):
        x = x_ref[...]
        d = x - jnp.mean(x, axis=-1, keepdims=True)
        var = jnp.mean(d * d, axis=-1, keepdims=True)
        dn = dn_ref[...]
        eps = jnp.concatenate([1e-5 * dn * dn, jnp.full((RET_HEADS * CHUNK, HEAD_V), 1e-5, F32)], axis=0)
        y = d * lax.rsqrt(var + eps)
        for h in range(n_heads):
            cols = slice(h * HEAD_V, (h + 1) * HEAD_V)
            if h < MLSTM_HEADS:
                gate = _sigmoid(proj_ref[rows, H_MO + h * HEAD_V:H_MO + (h + 1) * HEAD_V])
            else:
                hr = h - MLSTM_HEADS
                gate = _silu(proj_ref[rows, H_RG + hr * HEAD_V:H_RG + (hr + 1) * HEAD_V])
            ycat_ref[rows, cols] = (gate * (y[h * CHUNK:(h + 1) * CHUNK] * nw_ref[:, cols])).astype(BF16)

    def chunk_body(c, carry):
        rows = pl.ds(pl.multiple_of(c * CHUNK, CHUNK), CHUNK)

        gi = proj_ref[rows, H_GI:H_GI + LANES] + gbias_ref[0:1, :]
        gf = proj_ref[rows, H_GF:H_GF + LANES] + gbias_ref[1:2, :]
        bc = _token_cumsum(_log_sigmoid(gf))
        r = gi - bc
        m_prev = mst_ref[0:1, :]
        mu = jnp.maximum(m_prev, _token_cummax(r))
        g_row = bc[CHUNK - 1:CHUNK, :]
        m_new = jnp.maximum(g_row + m_prev, jnp.max(g_row + r, axis=0, keepdims=True))
        s_old = jnp.exp(g_row + m_prev - m_new)
        wt = jnp.exp(g_row + r - m_new).T
        rt = (r - m_prev).T
        mst_ref[0:1, :] = m_new
        alpha = mu - m_prev
        beta = bc + mu

        mq = (proj_ref[rows, H_MQ:H_MQ + QK_ALL] * 0.125).astype(BF16)
        mkt_f = proj_ref[rows, H_MK:H_MK + QK_ALL].T
        qhs = [jnp.where(lane_qk // HEAD_QK == h, mq, jnp.zeros_like(mq)) for h in range(MLSTM_HEADS)]
        scores = jnp.dot(jnp.concatenate(qhs, axis=0), mkt_f.astype(BF16), preferred_element_type=F32)
        for h in range(MLSTM_HEADS):
            hrows = slice(h * CHUNK, (h + 1) * CHUNK)
            alpha_b = _lane_replicate(alpha, h)
            beta_b = _lane_replicate(beta, h)
            c_prev = cst_ref[h]
            vext = jnp.concatenate(
                [proj_ref[rows, H_MV + h * HEAD_V:H_MV + (h + 1) * HEAD_V].astype(BF16), ones_v], axis=1)

            dmat = jnp.exp(jnp.where(causal, rt[h:h + 1, :] - alpha_b, neg_inf))
            s = scores[hrows] * dmat
            kw = (mkt_f[h * HEAD_QK:(h + 1) * HEAD_QK] * wt[h:h + 1, :]).astype(BF16)
            both = jnp.dot(jnp.concatenate([s.astype(BF16), kw], axis=0), vext, preferred_element_type=F32)
            inter = jnp.dot(qhs[h], _embed_rows(c_prev.astype(BF16), [h * HEAD_QK]),
                            preferred_element_type=F32)
            si_b = jnp.exp(-alpha_b)
            num_ext = both[:CHUNK] + jnp.concatenate([si_b, si_b], axis=1) * inter
            x_ref[hrows, :] = num_ext[:, :HEAD_V]
            dn_ref[hrows, :] = jnp.maximum(jnp.abs(num_ext[:, HEAD_V:]), jnp.exp(-beta_b))
            cst_ref[h] = s_old[:, h:h + 1] * c_prev + both[CHUNK:]

        cs, sn = cos_ref[rows, :], sin_ref[rows, :]
        q1, q2 = proj_ref[rows, H_RQ:H_RQ + LANES], proj_ref[rows, H_RQ + LANES:H_RQ + 2 * LANES]
        k1, k2 = proj_ref[rows, H_RK:H_RK + LANES], proj_ref[rows, H_RK + LANES:H_RK + 2 * LANES]
        rq = (jnp.concatenate([q1 * cs - q2 * sn, q1 * sn + q2 * cs], axis=1) * 0.125).astype(BF16)
        rkt_f = jnp.concatenate([k1 * cs - k2 * sn, k1 * sn + k2 * cs], axis=1).T
        qhs = [jnp.where((lane_qk % LANES) // half == h, rq, jnp.zeros_like(rq)) for h in range(RET_HEADS)]
        scores = jnp.dot(jnp.concatenate(qhs, axis=0), rkt_f.astype(BF16), preferred_element_type=F32)
        for h in range(RET_HEADS):
            hrows = slice((MLSTM_HEADS + h) * CHUNK, (MLSTM_HEADS + h + 1) * CHUNK)
            v = proj_ref[rows, H_RV + h * HEAD_V:H_RV + (h + 1) * HEAD_V].astype(BF16)
            r_prev = rst_ref[h]
            s = scores[h * CHUNK:(h + 1) * CHUNK] * dec_ref[h]
            k_own = jnp.concatenate([rkt_f[h * half:(h + 1) * half], rkt_f[LANES + h * half:LANES + (h + 1) * half]],
                                    axis=0)
            kw = (k_own * rrow_ref[h:h + 1, :]).astype(BF16)
            both = jnp.dot(jnp.concatenate([s.astype(BF16), kw], axis=0), v, preferred_element_type=F32)
            inter = jnp.dot(qhs[h], _embed_rows(r_prev.astype(BF16), [h * half, LANES + h * half]),
                            preferred_element_type=F32)
            x_ref[hrows, :] = both[:CHUNK] + wq_ref[h] * inter
            rst_ref[h] = rrow_ref[RET_HEADS + h:RET_HEADS + h + 1, 0:1] * r_prev + both[CHUNK:]

        finish_chunk(rows)
        return carry

    lax.fori_loop(0, n_chunks, chunk_body, 0, unroll=4)
    o_ref[...] = h_ref[...] + jnp.dot(ycat_ref[...], wout_ref[...], preferred_element_type=F32)


def _hyb_core(proj, h, cos_t, sin_t, gbias, nw_row, dec, wq_rep, rrow, wout):
    b, s, d = h.shape
    ts = TS_CORE
    inner = wout.shape[0]
    n_heads = MLSTM_HEADS + RET_HEADS
    return pl.pallas_call(
        _hyb_core_kernel,
        out_shape=jax.ShapeDtypeStruct((b, s, d), F32),
        grid=(b, s // ts),
        in_specs=[
            pl.BlockSpec((None, ts, H_PROJ), lambda bi, i: (bi, i, 0)),
            pl.BlockSpec((None, ts, d), lambda bi, i: (bi, i, 0)),
            pl.BlockSpec((ts, LANES), lambda bi, i: (i, 0)),
            pl.BlockSpec((ts, LANES), lambda bi, i: (i, 0)),
            _resident((SUBLANES, LANES)),
            _resident((1, inner)),
            _resident((RET_HEADS, CHUNK, CHUNK)),
            _resident((RET_HEADS, CHUNK, HEAD_V)),
            _resident((SUBLANES, LANES)),
            _resident((inner, d)),
        ],
        out_specs=pl.BlockSpec((None, ts, d), lambda bi, i: (bi, i, 0)),
        scratch_shapes=[
            pltpu.VMEM((MLSTM_HEADS, HEAD_QK, 2 * HEAD_V), F32),
            pltpu.VMEM((SUBLANES, LANES), F32),
            pltpu.VMEM((RET_HEADS, HEAD_QK, HEAD_V), F32),
            pltpu.VMEM((n_heads * CHUNK, HEAD_V), F32),
            pltpu.VMEM((MLSTM_HEADS * CHUNK, HEAD_V), F32),
            pltpu.VMEM((ts, inner), BF16),
        ],
        compiler_params=_compiler_params(("parallel", "arbitrary")),
        name="hybrid_core",
    )(proj, h, cos_t, sin_t, gbias, nw_row, dec, wq_rep, rrow, wout)


def _ssd_core_kernel(proj_ref, h_ref, dtb_ref, alog_ref, dsk_ref, nw_ref, wout_ref, o_ref, st_ref, ycat_ref):
    @pl.when(pl.program_id(1) == 0)
    def _init():
        st_ref[...] = jnp.zeros_like(st_ref)

    causal = _causal_mask()
    lane_g = lax.broadcasted_iota(jnp.int32, (1, SSD_GW), 1)
    neg_inf = jnp.float32(-jnp.inf)
    a_row = -jnp.exp(alog_ref[...])
    n_chunks = proj_ref.shape[0] // CHUNK

    def chunk_body(c, carry):
        rows = pl.ds(pl.multiple_of(c * CHUNK, CHUNK), CHUNK)

        dtp = _softplus(proj_ref[rows, S_DT:S_DT + LANES] + dtb_ref[...])
        cs = _token_cumsum(dtp * a_row)
        cst = cs.T
        dtt = dtp.T
        toendt = jnp.exp(cst[:, CHUNK - 1:CHUNK] - cst) * dtt
        dec_last = jnp.exp(cs[CHUNK - 1:CHUNK, :])

        for g in range(SSD_GROUPS):
            gcols = slice(g * SSD_GW, (g + 1) * SSD_GW)
            bg = proj_ref[rows, S_B + g * SSD_N:S_B + (g + 1) * SSD_N]
            cg = proj_ref[rows, S_C + g * SSD_N:S_C + (g + 1) * SSD_N]
            xg = proj_ref[rows, S_X + g * SSD_GW:S_X + (g + 1) * SSD_GW]
            cbm = lax.dot_general(cg.astype(BF16), bg.astype(BF16), (((1,), (1,)), ((), ())),
                                  preferred_element_type=F32)
            bgt = bg.T
            xb = xg.astype(BF16)
            st_prev = st_ref[g]
            hb = st_prev.astype(BF16)
            decay_row = jnp.zeros((1, SSD_GW), F32)
            lhs, rhs, st_lhs, st_rhs = [], [], [], []
            for j in range(SSD_REP):
                hh = g * SSD_REP + j
                jmask = lane_g // SSD_P == j
                xm = jnp.where(jmask, xb, jnp.zeros_like(xb))
                csb = _lane_replicate(cs, hh)
                seg = jnp.exp(jnp.where(causal, csb - cst[hh:hh + 1, :], neg_inf))
                lhs += [(cbm * seg * dtt[hh:hh + 1, :]).astype(BF16), (cg * jnp.exp(csb)).astype(BF16)]
                rhs += [xm, jnp.where(jmask, hb, jnp.zeros_like(hb))]
                st_lhs.append((bgt * toendt[hh:hh + 1, :]).astype(BF16))
                st_rhs.append(xm)
                decay_row = jnp.where(jmask, dec_last[:, hh:hh + 1], decay_row)
            yg = dsk_ref[:, gcols] * xg + jnp.dot(jnp.concatenate(lhs, axis=1), jnp.concatenate(rhs, axis=0),
                                                  preferred_element_type=F32)
            st_ref[g] = st_prev * decay_row + jnp.dot(jnp.concatenate(st_lhs, axis=1),
                                                      jnp.concatenate(st_rhs, axis=0), preferred_element_type=F32)
            yz = yg * _silu(proj_ref[rows, S_Z + g * SSD_GW:S_Z + (g + 1) * SSD_GW])
            ycat_ref[rows, gcols] = _rmsnorm(yz, nw_ref[:, gcols], 1e-5).astype(BF16)
        return carry

    lax.fori_loop(0, n_chunks, chunk_body, 0)
    o_ref[...] = h_ref[...] + jnp.dot(ycat_ref[...], wout_ref[...], preferred_element_type=F32)


def _ssd_core(proj, h, dtb, alog, dsk, nw_row, wout):
    b, s, d = h.shape
    ts = TS_CORE
    return pl.pallas_call(
        _ssd_core_kernel,
        out_shape=jax.ShapeDtypeStruct((b, s, d), F32),
        grid=(b, s // ts),
        in_specs=[
            pl.BlockSpec((None, ts, S_PROJ), lambda bi, i: (bi, i, 0)),
            pl.BlockSpec((None, ts, d), lambda bi, i: (bi, i, 0)),
            _resident((1, LANES)),
            _resident((1, LANES)),
            _resident((1, SSD_INNER)),
            _resident((1, SSD_INNER)),
            _resident((SSD_INNER, d)),
        ],
        out_specs=pl.BlockSpec((None, ts, d), lambda bi, i: (bi, i, 0)),
        scratch_shapes=[
            pltpu.VMEM((SSD_GROUPS, SSD_N, SSD_GW), F32),
            pltpu.VMEM((ts, SSD_INNER), BF16),
        ],
        compiler_params=_compiler_params(("parallel", "arbitrary")),
        name="ssd_core",
    )(proj, h, dtb, alog, dsk, nw_row, wout)


def _pad_lanes(row, width=LANES):
    return jnp.pad(row, ((0, 0), (0, width - row.shape[1])))


def _hybrid_layer(h, nw, w_in, i_bias, f_bias, m_norm_w, r_norm_w, w_out, cos_t, sin_t, dec, wq_rep, rrow):
    b, s, d = h.shape
    mq, mk, mv, mi, mf, mo, rq, rk, rv, rg = jnp.split(
        w_in, (256, 512, 1024, 1028, 1032, 1544, 1800, 2056, 2568), axis=1)
    perm = jnp.concatenate([
        (jnp.arange(LANES) // 32) * 64 + jnp.arange(LANES) % 32,
        (jnp.arange(LANES) // 32) * 64 + 32 + jnp.arange(LANES) % 32])
    gate_cols = _pad_lanes
    w_cat = jnp.concatenate([mq, mk, mv, mo, rq[:, perm], rk[:, perm], rv, rg, gate_cols(mi), gate_cols(mf)],
                            axis=1).astype(BF16)
    gbias = jnp.pad(jnp.concatenate([gate_cols(i_bias[None, :]), gate_cols(f_bias[None, :])], axis=0),
                    ((0, SUBLANES - 2), (0, 0)))
    proj = _norm_matmul(h.reshape(b * s, d), nw[None, :], w_cat).reshape(b, s, H_PROJ)
    nw_row = jnp.concatenate([m_norm_w, r_norm_w])[None, :]
    return _hyb_core(proj, h, cos_t, sin_t, gbias, nw_row, dec, wq_rep, rrow, w_out.astype(BF16))


def _ssd_layer(h, nw, w_in, conv_w, conv_b, dt_bias, a_log, d_skip, norm_w, w_out):
    b, s, d = h.shape
    w_cat = jnp.pad(w_in, ((0, 0), (0, S_PROJ - w_in.shape[1]))).astype(BF16)
    proj = _ssd_proj(h, nw[None, :], w_cat, conv_w, conv_b[None, :])
    dsk = jnp.repeat(d_skip, SSD_P)[None, :]
    return _ssd_core(proj, h, _pad_lanes(dt_bias[None, :]), _pad_lanes(a_log[None, :]), dsk, norm_w[None, :],
                     w_out.astype(BF16))


def _position_tables(seq):
    dk = 64
    inv = ROPE_BASE ** (-jnp.arange(0, dk, 2, dtype=F32) / dk)
    ang = jnp.arange(seq, dtype=F32)[:, None] * inv[None, :]
    cos_t = jnp.tile(jnp.cos(ang), (1, RET_HEADS))
    sin_t = jnp.tile(jnp.sin(ang), (1, RET_HEADS))
    log_gamma = jnp.log(1.0 - 2.0 ** (-5.0 - jnp.arange(RET_HEADS, dtype=F32)))
    idx = jnp.arange(CHUNK, dtype=F32)
    rel = idx[:, None] - idx[None, :]
    dec = jnp.where((rel >= 0)[None], jnp.exp(jnp.maximum(rel, 0.0)[None] * log_gamma[:, None, None]), 0.0)
    w_q = jnp.exp((idx + 1.0)[:, None] * log_gamma)
    w_k = jnp.exp((CHUNK - 1.0 - idx)[:, None] * log_gamma)
    chunk_decay = jnp.broadcast_to(jnp.exp(CHUNK * log_gamma)[:, None], (RET_HEADS, CHUNK))
    tok = _token_of_row(jnp.arange(CHUNK))
    wq_rep = jnp.broadcast_to(w_q[tok].T[:, :, None], (RET_HEADS, CHUNK, HEAD_V))
    rrow = jnp.concatenate([w_k[tok].T, chunk_decay], axis=0)
    return _interleave_tokens(cos_t, 0), _interleave_tokens(sin_t, 0), dec[:, tok][:, :, tok], wq_rep, rrow


def _interleave_tokens(a, axis):
    shape = a.shape
    split = shape[:axis] + (shape[axis] // CHUNK, SUBLANES, CHUNK // SUBLANES) + shape[axis + 1:]
    return jnp.swapaxes(a.reshape(split), axis + 1, axis + 2).reshape(shape)


def _deinterleave_tokens(a, axis):
    shape = a.shape
    split = shape[:axis] + (shape[axis] // CHUNK, CHUNK // SUBLANES, SUBLANES) + shape[axis + 1:]
    return jnp.swapaxes(a.reshape(split), axis + 1, axis + 2).reshape(shape)


@jax.jit
def kernel(x, norm_mix_w, norm_mlp_w, hyb_w_in, mlstm_i_bias, mlstm_f_bias, mlstm_norm_w, ret_norm_w, hyb_w_out,
           ssd_w_in, ssd_conv_w, ssd_conv_b, ssd_dt_bias, ssd_a_log, ssd_d, ssd_norm_w, ssd_w_out, mlp_w1, mlp_w2,
           final_norm_w):
    b, s, d = x.shape
    depth = norm_mix_w.shape[0]
    cos_t, sin_t, dec, wq_rep, rrow = _position_tables(s)
    h = _interleave_tokens(x, 1)
    for layer in range(depth):
        j = layer // 2
        if layer % 2 == 0:
            h = _hybrid_layer(h, norm_mix_w[layer], hyb_w_in[j], mlstm_i_bias[j], mlstm_f_bias[j], mlstm_norm_w[j],
                              ret_norm_w[j], hyb_w_out[j], cos_t, sin_t, dec, wq_rep, rrow)
        else:
            h = _ssd_layer(h, norm_mix_w[layer], ssd_w_in[j], ssd_conv_w[j], ssd_conv_b[j], ssd_dt_bias[j],
                           ssd_a_log[j], ssd_d[j], ssd_norm_w[j], ssd_w_out[j])
        h = _mlp(h.reshape(b * s, d), norm_mlp_w[layer][None, :], mlp_w1[layer].astype(BF16),
                 mlp_w2[layer].astype(BF16), final_norm_w[None, :], layer == depth - 1).reshape(b, s, d)
    return _deinterleave_tokens(h, 1)
```
